```python
import math
import jax, jax.numpy as jnp
from jax import lax
import numpy as np

D_MODEL = 2048
BATCH = 4
SEQ = 4096
DEPTH = 1

MIX_WIDTH = D_MODEL
ATTN_WIDTH = MIX_WIDTH // 2
ATTN_HEADS = 8
HEAD_DIM = ATTN_WIDTH // ATTN_HEADS
Q_LORA = D_MODEL // 4
KV_LORA = D_MODEL // 8
IDX_HEADS = 16
IDX_DIM = 64
IDX_TOPK_MAX = 256
CONV_WIDTH = MIX_WIDTH - ATTN_WIDTH
CONV_GROUPS = 8
CONV_GROUP_DIM = CONV_WIDTH // CONV_GROUPS
CONV_K = 3
D_FF = 4 * D_MODEL
N_BUCKETS = 32
MAX_DISTANCE = 128
Q_BLOCK = 128
EPS = 1e-6
IN_COLS = Q_LORA + KV_LORA + IDX_DIM + IDX_HEADS + 3 * CONV_WIDTH

kernel_name = 'hybrid_dsa_shortconv_adaln_layer'


def rms_norm(x, g):
    x32 = x.astype(jnp.float32)
    y = x32 * lax.rsqrt(jnp.mean(x32 * x32, axis=-1, keepdims=True) + EPS)
    return (y * g.astype(jnp.float32)).astype(x.dtype)


def layer_norm(x, g, b):
    x32 = x.astype(jnp.float32)
    mu = jnp.mean(x32, axis=-1, keepdims=True)
    xc = x32 - mu
    y = xc * lax.rsqrt(jnp.mean(xc * xc, axis=-1, keepdims=True) + EPS)
    return (y * g.astype(jnp.float32) + b.astype(jnp.float32)).astype(x.dtype)


def modulate(h, shift, scale):
    return h * (1 + scale[:, None, :]) + shift[:, None, :]


def t5_bucket(n):
    max_exact = N_BUCKETS // 2
    nf = jnp.maximum(n, 1).astype(jnp.float32)
    large = max_exact + (jnp.log(nf / max_exact) / math.log(MAX_DISTANCE / max_exact)
                         * (N_BUCKETS - max_exact)).astype(jnp.int32)
    large = jnp.minimum(large, N_BUCKETS - 1)
    return jnp.where(n < max_exact, n, large)


def causal_short_conv(u, w, b):
    S = u.shape[1]
    up = jnp.pad(u, ((0, 0), (CONV_K - 1, 0), (0, 0)))
    y = b
    for j in range(CONV_K):
        y = y + w[j] * up[:, j:j + S]
    return y


def dsa_attention(cq, ckv, k_idx, w_idx, w_uq, w_uk, q_abs_norm_g, w_iq, rel_bias):
    Bn, S, _ = cq.shape
    topk = min(IDX_TOPK_MAX, S // 4)
    nb = S // Q_BLOCK
    q = (cq @ w_uq).reshape(Bn, S, ATTN_HEADS, HEAD_DIM)
    q_abs = jnp.einsum('bshd,hdc->bshc', q, w_uk)
    q_abs = rms_norm(q_abs, q_abs_norm_g)
    iq = (cq @ w_iq).reshape(Bn, S, IDX_HEADS, IDX_DIM)
    w_idx = w_idx * (IDX_HEADS ** -0.5 * IDX_DIM ** -0.5)

    def to_blocks(a):
        return a.reshape((Bn, nb, Q_BLOCK) + a.shape[2:]).swapaxes(0, 1)

    key_pos = jnp.arange(S, dtype=jnp.int32)
    attn_scale = KV_LORA ** -0.5
    neg = jnp.finfo(jnp.float32).min

    def block(args):
        qa, iqb, wb, blk = args
        q_pos = blk * Q_BLOCK + jnp.arange(Q_BLOCK, dtype=jnp.int32)
        lg = jnp.einsum('bqhd,bsd->bqhs', iqb, k_idx)
        score = jnp.einsum('bqhs,bqh->bqs', jax.nn.relu(lg), wb).astype(jnp.float32)
        causal = key_pos[None, :] <= q_pos[:, None]
        score = jnp.where(causal[None], score, neg)
        _, sel = lax.top_k(score, topk)
        valid = sel <= q_pos[None, :, None]
        kv_sel = jax.vmap(lambda kv, i: kv[i])(ckv, sel)
        logits = jnp.einsum('bqhc,bqkc->bhqk', qa, kv_sel).astype(jnp.float32) * attn_scale
        bucket = t5_bucket(jnp.maximum(q_pos[None, :, None] - sel, 0))
        bias = rel_bias[bucket]
        logits = logits + jnp.moveaxis(bias, -1, 1).astype(jnp.float32)
        logits = jnp.where(valid[:, None], logits, neg)
        p = jax.nn.softmax(logits, axis=-1).astype(kv_sel.dtype)
        return jnp.einsum('bhqk,bqkc->bqhc', p, kv_sel)

    o = lax.map(block, (to_blocks(q_abs), to_blocks(iq), to_blocks(w_idx),
                        jnp.arange(nb, dtype=jnp.int32)))
    return o.swapaxes(0, 1).reshape(Bn, S, ATTN_HEADS, KV_LORA)


def setup_inputs(seed: int = 0) -> dict:
    key = jax.random.key(seed)
    ks = jax.random.split(key, 32)
    f32 = jnp.float32

    def nrm(k, shape, scale):
        return jax.random.normal(k, shape, f32) * scale

    def gain(k, shape):
        return 1.0 + 0.02 * jax.random.normal(k, shape, f32)

    L = DEPTH
    return {
        'x': nrm(ks[0], (BATCH, SEQ, D_MODEL), 1.0),
        'c': nrm(ks[1], (BATCH, D_MODEL), 1.0),
        'w_ada': nrm(ks[2], (L, D_MODEL, 6 * D_MODEL), D_MODEL ** -0.5),
        'b_ada': nrm(ks[3], (L, 6 * D_MODEL), 0.02),
        'norm1_g': gain(ks[4], (L, D_MODEL)),
        'norm2_g': gain(ks[5], (L, D_MODEL)),
        'w_in': nrm(ks[6], (L, D_MODEL, IN_COLS), D_MODEL ** -0.5),
        'cq_norm_g': gain(ks[7], (L, Q_LORA)),
        'w_uq': nrm(ks[8], (L, Q_LORA, ATTN_HEADS * HEAD_DIM), Q_LORA ** -0.5),
        'w_uk': nrm(ks[9], (L, ATTN_HEADS, HEAD_DIM, KV_LORA), HEAD_DIM ** -0.5),
        'kv_norm_g': gain(ks[10], (L, KV_LORA)),
        'q_abs_norm_g': gain(ks[11], (L, KV_LORA)),
        'w_uv': nrm(ks[12], (L, ATTN_HEADS, KV_LORA, HEAD_DIM), KV_LORA ** -0.5),
        'w_iq': nrm(ks[13], (L, Q_LORA, IDX_HEADS * IDX_DIM), Q_LORA ** -0.5),
        'idx_k_norm_g': gain(ks[14], (L, IDX_DIM)),
        'idx_k_norm_b': nrm(ks[15], (L, IDX_DIM), 0.02),
        'rel_bias': nrm(ks[16], (N_BUCKETS, ATTN_HEADS), 0.5),
        'conv_w': nrm(ks[17], (L, CONV_K, CONV_WIDTH), CONV_K ** -0.5),
        'conv_b': nrm(ks[18], (L, CONV_WIDTH), 0.02),
        'attn_out_norm_g': gain(ks[19], (L, ATTN_HEADS, HEAD_DIM)),
        'conv_out_norm_g': gain(ks[20], (L, CONV_GROUPS, CONV_GROUP_DIM)),
        'w_out': nrm(ks[21], (L, MIX_WIDTH, D_MODEL), MIX_WIDTH ** -0.5),
        'w_mlp1': nrm(ks[22], (L, D_MODEL, D_FF), D_MODEL ** -0.5),
        'b_mlp1': nrm(ks[23], (L, D_FF), 0.02),
        'w_mlp2': nrm(ks[24], (L, D_FF, D_MODEL), D_FF ** -0.5),
        'b_mlp2': nrm(ks[25], (L, D_MODEL), 0.02),
    }


def reference(x, c, w_ada, b_ada, norm1_g, norm2_g, w_in, cq_norm_g, w_uq, w_uk,
              kv_norm_g, q_abs_norm_g, w_uv, w_iq, idx_k_norm_g, idx_k_norm_b, rel_bias,
              conv_w, conv_b, attn_out_norm_g, conv_out_norm_g, w_out, w_mlp1, b_mlp1,
              w_mlp2, b_mlp2):
    Bn, S, D = x.shape
    c_act = jax.nn.silu(c)
    o0 = Q_LORA
    o1 = o0 + KV_LORA
    o2 = o1 + IDX_DIM
    o3 = o2 + IDX_HEADS
    o4 = o3 + CONV_WIDTH
    o5 = o4 + CONV_WIDTH
    for l in range(DEPTH):
        mod = c_act @ w_ada[l] + b_ada[l]
        sh1, sc1, g1, sh2, sc2, g2 = jnp.split(mod, 6, axis=-1)

        h = modulate(rms_norm(x, norm1_g[l]), sh1, sc1)
        proj = h @ w_in[l]
        cq = rms_norm(proj[..., :o0], cq_norm_g[l])
        ckv = rms_norm(proj[..., o0:o1], kv_norm_g[l])
        k_idx = layer_norm(proj[..., o1:o2], idx_k_norm_g[l], idx_k_norm_b[l])
        w_idx = proj[..., o2:o3]
        gate_b = proj[..., o3:o4]
        gate_c = proj[..., o4:o5]
        h_conv = proj[..., o5:]

        o_lat = dsa_attention(cq, ckv, k_idx, w_idx, w_uq[l], w_uk[l], q_abs_norm_g[l],
                              w_iq[l], rel_bias)
        y_attn = jnp.einsum('bshc,hcv->bshv', o_lat, w_uv[l])
        y_attn = rms_norm(y_attn, attn_out_norm_g[l])

        y_conv = gate_b * causal_short_conv(gate_c * h_conv, conv_w[l], conv_b[l])
        y_conv = rms_norm(y_conv.reshape(Bn, S, CONV_GROUPS, CONV_GROUP_DIM), conv_out_norm_g[l])

        y_mix = jnp.concatenate([y_attn.reshape(Bn, S, ATTN_WIDTH),
                                 y_conv.reshape(Bn, S, CONV_WIDTH)], axis=-1)
        x = x + g1[:, None, :] * (y_mix @ w_out[l])

        h2 = modulate(rms_norm(x, norm2_g[l]), sh2, sc2)
        a = jnp.square(jax.nn.relu(h2 @ w_mlp1[l] + b_mlp1[l]))
        x = x + g2[:, None, :] * (a @ w_mlp2[l] + b_mlp2[l])
    return x
```

```python
import functools
import math

import numpy as np
import jax
import jax.numpy as jnp
from jax import lax
from jax.experimental import pallas as pl
from jax.experimental.pallas import tpu as pltpu

F32 = jnp.float32
BF16 = jnp.bfloat16

ATTN_HEADS = 8
HEAD_DIM = 128
Q_LORA = 512
KV_LORA = 256
IDX_HEADS = 16
IDX_DIM = 64
IDX_TOPK_MAX = 256
CONV_WIDTH = 1024
CONV_GROUPS = 8
CONV_GROUP_DIM = 128
CONV_K = 3
N_BUCKETS = 32
MAX_DISTANCE = 128
EPS = 1e-6

LANES = 128
KEY_BLOCK = 256
MASK_NEG = -1e30
M_INIT = -3e38
INT_MIN = -(2 ** 31)
VMEM_LIMIT = 56 * 1024 * 1024

C_CQ = 0
C_KV = C_CQ + Q_LORA
C_KI = C_KV + KV_LORA
C_WI = C_KI + LANES
C_GB = C_WI + LANES
C_GC = C_GB + CONV_WIDTH
C_HC = C_GC + CONV_WIDTH
IN_COLS_PAD = C_HC + CONV_WIDTH


def _dot(a, b):
    return jnp.dot(a, b, preferred_element_type=F32)


def _rms(x, g):
    return x * lax.rsqrt(jnp.mean(x * x, axis=-1, keepdims=True) + EPS) * g


def _params(sem):
    return pltpu.CompilerParams(dimension_semantics=sem, vmem_limit_bytes=VMEM_LIMIT)


def _const_spec(shape):
    n = len(shape)
    return pl.BlockSpec(shape, lambda *_: (0,) * n, pipeline_mode=pl.Buffered(1))


def _mod_kernel(c_ref, w_ref, b_ref, o_ref):
    c = c_ref[...]
    c_act = c / (1.0 + jnp.exp(-c))
    o_ref[...] = _dot(c_act, w_ref[...]) + b_ref[...]


def _mod_call(c8, w_ada, b_ada):
    d, n = w_ada.shape
    tn = 1024
    return pl.pallas_call(
        _mod_kernel,
        grid=(n // tn,),
        in_specs=[pl.BlockSpec((8, d), lambda j: (0, 0)),
                  pl.BlockSpec((d, tn), lambda j: (0, j)),
                  pl.BlockSpec((1, tn), lambda j: (0, j))],
        out_specs=pl.BlockSpec((8, tn), lambda j: (0, j)),
        out_shape=jax.ShapeDtypeStruct((8, n), F32),
        compiler_params=_params(("parallel",)),
        name="mod",
    )(c8, w_ada, b_ada)


def _in_kernel(x_ref, mod_ref, g1_ref, w_ref, cqg_ref, kvg_ref, kig_ref, kib_ref, cw_ref, cb_ref,
               cg_ref, cq_ref, ckv_ref, ki_ref, wi_ref, yc_ref, ubuf_ref, *, tm):
    x = x_ref[0]
    h = _rms(x, g1_ref[...]) * (1.0 + mod_ref[0, 1:2, :]) + mod_ref[0, 0:1, :]
    hb = h.astype(BF16)

    cq_ref[0] = _rms(_dot(hb, w_ref[:, C_CQ:C_CQ + Q_LORA]), cqg_ref[...]).astype(BF16)
    ckv_ref[0] = _rms(_dot(hb, w_ref[:, C_KV:C_KV + KV_LORA]), kvg_ref[...]).astype(BF16)

    ki = _dot(hb, w_ref[:, C_KI:C_KI + LANES])
    real = lax.broadcasted_iota(jnp.int32, (1, LANES), 1) < IDX_DIM
    mu = jnp.sum(jnp.where(real, ki, 0.0), axis=-1, keepdims=True) * (1.0 / IDX_DIM)
    xc = jnp.where(real, ki - mu, 0.0)
    var = jnp.sum(xc * xc, axis=-1, keepdims=True) * (1.0 / IDX_DIM)
    ki_ref[0] = (xc * lax.rsqrt(var + EPS) * kig_ref[...] + kib_ref[...]).astype(BF16)

    wi_ref[0] = _dot(hb, w_ref[:, C_WI:C_WI + LANES]) * (IDX_HEADS ** -0.5 * IDX_DIM ** -0.5)

    @pl.when(pl.program_id(1) == 0)
    def _():
        ubuf_ref[0:8, :] = jnp.zeros((8, CONV_WIDTH), F32)

    half = CONV_WIDTH // 2
    for c0 in (0, half):
        gc = _dot(hb, w_ref[:, C_GC + c0:C_GC + c0 + half])
        hc = _dot(hb, w_ref[:, C_HC + c0:C_HC + c0 + half])
        ubuf_ref[8:8 + tm, c0:c0 + half] = gc * hc
    for c0 in (0, half):
        gb = _dot(hb, w_ref[:, C_GB + c0:C_GB + c0 + half])
        y = cb_ref[:, c0:c0 + half]
        for j in range(CONV_K):
            lo = 8 - (CONV_K - 1) + j
            y = y + cw_ref[j:j + 1, c0:c0 + half] * ubuf_ref[lo:lo + tm, c0:c0 + half]
        y = gb * y
        for g in range(half // CONV_GROUP_DIM):
            l0 = g * CONV_GROUP_DIM
            yg = y[:, l0:l0 + CONV_GROUP_DIM]
            yc_ref[0, :, c0 + l0:c0 + l0 + CONV_GROUP_DIM] = _rms(
                yg, cg_ref[:, c0 + l0:c0 + l0 + CONV_GROUP_DIM]).astype(BF16)
    ubuf_ref[0:8, :] = ubuf_ref[tm:tm + 8, :]


def _in_call(x, mod3, norm1_g, w_in_r, cq_g, kv_g, ki_g, ki_b, conv_w, conv_b, conv_g, tm):
    bn, s, d = x.shape
    kern = functools.partial(_in_kernel, tm=tm)
    tok = lambda w: pl.BlockSpec((1, tm, w), lambda b, i: (b, i, 0))
    return pl.pallas_call(
        kern,
        grid=(bn, s // tm),
        in_specs=[tok(d),
                  pl.BlockSpec((1, 6, d), lambda b, i: (b, 0, 0)),
                  _const_spec((1, d)),
                  _const_spec((d, IN_COLS_PAD)),
                  _const_spec((1, Q_LORA)),
                  _const_spec((1, KV_LORA)),
                  _const_spec((1, LANES)),
                  _const_spec((1, LANES)),
                  _const_spec((CONV_K, CONV_WIDTH)),
                  _const_spec((1, CONV_WIDTH)),
                  _const_spec((1, CONV_WIDTH))],
        out_specs=[tok(Q_LORA), tok(KV_LORA), tok(LANES), tok(LANES), tok(CONV_WIDTH)],
        out_shape=[jax.ShapeDtypeStruct((bn, s, Q_LORA), BF16),
                   jax.ShapeDtypeStruct((bn, s, KV_LORA), BF16),
                   jax.ShapeDtypeStruct((bn, s, LANES), BF16),
                   jax.ShapeDtypeStruct((bn, s, LANES), F32),
                   jax.ShapeDtypeStruct((bn, s, CONV_WIDTH), BF16)],
        scratch_shapes=[pltpu.VMEM((tm + 8, CONV_WIDTH), F32)],
        compiler_params=_params(("parallel", "arbitrary")),
        name="in_proj",
    )(x, mod3, norm1_g, w_in_r, cq_g, kv_g, ki_g, ki_b, conv_w, conv_b, conv_g)


def _q_kernel(cq_ref, wuq_ref, wuk_ref, qg_ref, wiq_ref, qa_ref, iq_ref):
    cq = cq_ref[0]
    q = _dot(cq, wuq_ref[...]).astype(BF16)
    for h in range(ATTN_HEADS):
        qa = _dot(q[:, h * HEAD_DIM:(h + 1) * HEAD_DIM], wuk_ref[h])
        qa_ref[0, h] = (_rms(qa, qg_ref[...]) * (KV_LORA ** -0.5)).astype(BF16)
    for h in range(IDX_HEADS):
        iq_ref[0, h] = _dot(cq, wiq_ref[:, h * LANES:(h + 1) * LANES]).astype(BF16)


def _q_call(cq, w_uq, w_uk, qa_g, w_iq_pad, tq):
    bn, s, _ = cq.shape
    return pl.pallas_call(
        _q_kernel,
        grid=(bn, s // tq),
        in_specs=[pl.BlockSpec((1, tq, Q_LORA), lambda b, i: (b, i, 0)),
                  _const_spec((Q_LORA, ATTN_HEADS * HEAD_DIM)),
                  _const_spec((ATTN_HEADS, HEAD_DIM, KV_LORA)),
                  _const_spec((1, KV_LORA)),
                  _const_spec((Q_LORA, IDX_HEADS * LANES))],
        out_specs=[pl.BlockSpec((1, ATTN_HEADS, tq, KV_LORA), lambda b, i: (b, 0, i, 0)),
                   pl.BlockSpec((1, IDX_HEADS, tq, LANES), lambda b, i: (b, 0, i, 0))],
        out_shape=[jax.ShapeDtypeStruct((bn, ATTN_HEADS, s, KV_LORA), BF16),
                   jax.ShapeDtypeStruct((bn, IDX_HEADS, s, LANES), BF16)],
        compiler_params=_params(("parallel", "parallel")),
        name="q_proj",
    )(cq, w_uq, w_uk, qa_g, w_iq_pad)


IDX_QB = 128
IDX_RB = 32


def _idx_kernel(iq_ref, kt_ref, w_ref, bias_ref, lg_ref, key_ref, wb_ref, thr_ref, *, topk, nkb):
    i = pl.program_id(1)
    nch = (i + 2) // 2
    nrb = IDX_QB // IDX_RB

    w = w_ref[0]
    for h in range(IDX_HEADS):
        wb_ref[h] = jnp.broadcast_to(w[:, h:h + 1], (IDX_QB, LANES))
    iq = iq_ref[0].reshape(IDX_HEADS * IDX_QB, LANES)

    def score_chunk(c, carry):
        lg_ref[...] = _dot(iq, kt_ref[0, c])
        kpos = c * KEY_BLOCK + lax.broadcasted_iota(jnp.int32, (IDX_RB, KEY_BLOCK), 1)
        for rb in range(nrb):
            r0 = rb * IDX_RB
            acc = jnp.zeros((IDX_RB, KEY_BLOCK), F32)
            for h in range(IDX_HEADS):
                lg = jnp.maximum(lg_ref[h * IDX_QB + r0:h * IDX_QB + r0 + IDX_RB, :], 0.0)
                wb = wb_ref[h, r0:r0 + IDX_RB, :]
                acc = acc + lg * jnp.concatenate([wb, wb], axis=1)
            bits = pltpu.bitcast(acc, jnp.int32)
            key = bits ^ ((bits >> 31) & 0x7FFFFFFF)
            qpos = i * IDX_QB + r0 + lax.broadcasted_iota(jnp.int32, (IDX_RB, KEY_BLOCK), 0)
            key_ref[c, r0:r0 + IDX_RB, :] = jnp.where(kpos <= qpos, key, INT_MIN)
        return carry

    lax.fori_loop(0, nch, score_chunk, 0)

    for rb in range(nrb):
        r0 = rb * IDX_RB

        def count_ge(cand):
            def body(c, cnt):
                k = key_ref[c, r0:r0 + IDX_RB, :]
                one = jnp.where(k[:, :LANES] >= cand, 1, 0) + jnp.where(k[:, LANES:] >= cand, 1, 0)
                return cnt + one
            cnt = lax.fori_loop(0, nch, body, jnp.zeros((IDX_RB, LANES), jnp.int32))
            return jnp.sum(cnt.astype(F32), axis=-1, keepdims=True)

        zero = jnp.zeros((IDX_RB, LANES), jnp.int32)
        t0 = jnp.where(count_ge(zero) >= topk, zero, INT_MIN)

        def bit_step(j, t):
            cand = t + jnp.left_shift(jnp.int32(1), 30 - j)
            return jnp.where(count_ge(cand) >= topk, cand, t)

        t = lax.fori_loop(0, 31, bit_step, t0)
        thr_ref[r0:r0 + IDX_RB, :] = jnp.maximum(t, INT_MIN + 1)

    def write_chunk(c, carry):
        for rb in range(nrb):
            r0 = rb * IDX_RB
            t = thr_ref[r0:r0 + IDX_RB, :]
            k = key_ref[c, r0:r0 + IDX_RB, :]
            sel = k >= jnp.concatenate([t, t], axis=1)
            bias_ref[0, c, r0:r0 + IDX_RB, :] = jnp.where(sel, 0.0, MASK_NEG).astype(BF16)
        return carry

    lax.fori_loop(0, nch, write_chunk, 0)

    def fill_chunk(c, carry):
        bias_ref[0, c] = jnp.full((IDX_QB, KEY_BLOCK), MASK_NEG, BF16)
        return carry

    lax.fori_loop(nch, nkb, fill_chunk, 0)


def _idx_call(iq, kidx_t, widx, topk):
    bn, _, s, _ = iq.shape
    nkb = s // KEY_BLOCK
    kern = functools.partial(_idx_kernel, topk=topk, nkb=nkb)
    return pl.pallas_call(
        kern,
        grid=(bn, s // IDX_QB),
        in_specs=[pl.BlockSpec((1, IDX_HEADS, IDX_QB, LANES), lambda b, i: (b, 0, i, 0)),
                  pl.BlockSpec((1, nkb, LANES, KEY_BLOCK), lambda b, i: (b, 0, 0, 0)),
                  pl.BlockSpec((1, IDX_QB, LANES), lambda b, i: (b, i, 0))],
        out_specs=pl.BlockSpec((1, nkb, IDX_QB, KEY_BLOCK), lambda b, i: (b, 0, i, 0)),
        out_shape=jax.ShapeDtypeStruct((bn, nkb, s, KEY_BLOCK), BF16),
        scratch_shapes=[pltpu.VMEM((IDX_HEADS * IDX_QB, KEY_BLOCK), F32),
                        pltpu.VMEM((nkb, IDX_QB, KEY_BLOCK), jnp.int32),
                        pltpu.VMEM((IDX_HEADS, IDX_QB, LANES), F32),
                        pltpu.VMEM((IDX_QB, LANES), jnp.int32)],
        compiler_params=_params(("parallel", "arbitrary")),
        name="indexer",
    )(iq, kidx_t, widx)


ATT_QB = 256
ATT_RB = 64


def _attn_kernel(q_ref, kt_ref, v_ref, bias_ref, tb_ref, wuv_ref, og_ref, out_ref,
                 s_ref, p_ref, m_ref, l_ref, acc_ref):
    i = pl.program_id(1)
    rows = ATTN_HEADS * ATT_QB
    nrb = rows // ATT_RB
    rb_per_head = ATT_QB // ATT_RB
    nl = KEY_BLOCK // LANES

    q = q_ref[0].reshape(rows, KV_LORA)
    m_ref[...] = jnp.full((rows, LANES), M_INIT, F32)
    l_ref[...] = jnp.zeros((rows, LANES), F32)
    acc_ref[...] = jnp.zeros((rows, KV_LORA), F32)

    def block(kb, tb_col):
        s_ref[...] = _dot(q, kt_ref[0, kb])

        def row_block(rb, carry):
            r0 = pl.multiple_of(rb * ATT_RB, ATT_RB)
            qr0 = pl.multiple_of((rb % rb_per_head) * ATT_RB, ATT_RB)
            s = s_ref[pl.ds(r0, ATT_RB), :] + bias_ref[0, kb, pl.ds(qr0, ATT_RB), :].astype(F32)
            if tb_col is not None:
                s = s + tb_ref[rb // rb_per_head, pl.ds(qr0, ATT_RB), tb_col:tb_col + KEY_BLOCK]
            m_old = m_ref[pl.ds(r0, ATT_RB), :]
            m_new = jnp.maximum(m_old, jnp.max(s, axis=-1, keepdims=True))
            alpha = jnp.exp(m_old - m_new)
            ps = [jnp.exp(s[:, c * LANES:(c + 1) * LANES] - m_new) for c in range(nl)]
            p = jnp.concatenate(ps, axis=1)
            l_ref[pl.ds(r0, ATT_RB), :] = alpha * l_ref[pl.ds(r0, ATT_RB), :] + jnp.sum(
                p, axis=-1, keepdims=True)
            m_ref[pl.ds(r0, ATT_RB), :] = m_new
            p_ref[pl.ds(r0, ATT_RB), :] = p.astype(BF16)
            a = acc_ref[pl.ds(r0, ATT_RB), :]
            acc_ref[pl.ds(r0, ATT_RB), :] = a * jnp.concatenate([alpha] * (KV_LORA // LANES), axis=1)
            return carry

        lax.fori_loop(0, nrb, row_block, 0)
        acc_ref[...] += _dot(p_ref[...], v_ref[0, pl.ds(pl.multiple_of(kb * KEY_BLOCK, KEY_BLOCK), KEY_BLOCK), :])

    def far(kb, carry):
        block(kb, None)
        return carry

    lax.fori_loop(0, jnp.maximum(i - 1, 0), far, 0)

    @pl.when(i > 0)
    def _():
        block(i - 1, 0)

    block(i, KEY_BLOCK)

    for h in range(ATTN_HEADS):
        r0 = h * ATT_QB
        inv = 1.0 / l_ref[r0:r0 + ATT_QB, :]
        o = acc_ref[r0:r0 + ATT_QB, :] * jnp.concatenate([inv] * (KV_LORA // LANES), axis=1)
        y = _dot(o.astype(BF16), wuv_ref[h])
        out_ref[0, :, h * HEAD_DIM:(h + 1) * HEAD_DIM] = _rms(y, og_ref[h:h + 1, :]).astype(BF16)


def _attn_call(q_abs, ckv_t, ckv, bias, tb, w_uv, o_g):
    bn, _, s, _ = q_abs.shape
    nkb = s // KEY_BLOCK
    rows = ATTN_HEADS * ATT_QB
    return pl.pallas_call(
        _attn_kernel,
        grid=(bn, s // ATT_QB),
        in_specs=[pl.BlockSpec((1, ATTN_HEADS, ATT_QB, KV_LORA), lambda b, i: (b, 0, i, 0)),
                  pl.BlockSpec((1, nkb, KV_LORA, KEY_BLOCK), lambda b, i: (b, 0, 0, 0)),
                  pl.BlockSpec((1, s, KV_LORA), lambda b, i: (b, 0, 0)),
                  pl.BlockSpec((1, nkb, ATT_QB, KEY_BLOCK), lambda b, i: (b, 0, i, 0)),
                  _const_spec((ATTN_HEADS, ATT_QB, 2 * KEY_BLOCK)),
                  _const_spec((ATTN_HEADS, KV_LORA, HEAD_DIM)),
                  _const_spec((ATTN_HEADS, HEAD_DIM))],
        out_specs=pl.BlockSpec((1, ATT_QB, ATTN_HEADS * HEAD_DIM), lambda b, i: (b, i, 0)),
        out_shape=jax.ShapeDtypeStruct((bn, s, ATTN_HEADS * HEAD_DIM), BF16),
        scratch_shapes=[pltpu.VMEM((rows, KEY_BLOCK), F32),
                        pltpu.VMEM((rows, KEY_BLOCK), BF16),
                        pltpu.VMEM((rows, LANES), F32),
                        pltpu.VMEM((rows, LANES), F32),
                        pltpu.VMEM((rows, KV_LORA), F32)],
        compiler_params=_params(("parallel", "arbitrary")),
        name="attn",
    )(q_abs, ckv_t, ckv, bias, tb, w_uv, o_g)


def _out_kernel(x_ref, ya_ref, yc_ref, w_ref, mod_ref, g2_ref, x1_ref, h2_ref):
    half = w_ref.shape[0] // 2
    y = _dot(ya_ref[0], w_ref[0:half, :]) + _dot(yc_ref[0], w_ref[half:, :])
    x1 = x_ref[0] + mod_ref[0, 2:3, :] * y
    x1_ref[0] = x1
    h2_ref[0] = (_rms(x1, g2_ref[...]) * (1.0 + mod_ref[0, 4:5, :]) + mod_ref[0, 3:4, :]).astype(BF16)


def _out_call(x, y_attn, y_conv, w_out, mod3, norm2_g, tm):
    bn, s, d = x.shape
    half = y_attn.shape[-1]
    tok = lambda w: pl.BlockSpec((1, tm, w), lambda b, i: (b, i, 0))
    return pl.pallas_call(
        _out_kernel,
        grid=(bn, s // tm),
        in_specs=[tok(d), tok(half), tok(half),
                  _const_spec((2 * half, d)),
                  pl.BlockSpec((1, 6, d), lambda b, i: (b, 0, 0)),
                  _const_spec((1, d))],
        out_specs=[tok(d), tok(d)],
        out_shape=[jax.ShapeDtypeStruct((bn, s, d), F32),
                   jax.ShapeDtypeStruct((bn, s, d), BF16)],
        compiler_params=_params(("parallel", "parallel")),
        name="out_proj",
    )(x, y_attn, y_conv, w_out, mod3, norm2_g)


def _mlp_kernel(h2_ref, x1_ref, w1_ref, b1_ref, w2_ref, b2_ref, mod_ref, o_ref, acc_ref):
    f = pl.program_id(2)
    a = jnp.maximum(_dot(h2_ref[0], w1_ref[...]) + b1_ref[...], 0.0)
    contrib = _dot((a * a).astype(BF16), w2_ref[...])

    @pl.when(f == 0)
    def _():
        acc_ref[...] = contrib

    @pl.when(f > 0)
    def _():
        acc_ref[...] += contrib

    @pl.when(f == pl.num_programs(2) - 1)
    def _():
        o_ref[0] = x1_ref[0] + mod_ref[0, 5:6, :] * (acc_ref[...] + b2_ref[...])


def _mlp_call(h2, x1, w1, b1, w2, b2, mod3, tm, tf):
    bn, s, d = x1.shape
    dff = w1.shape[1]
    return pl.pallas_call(
        _mlp_kernel,
        grid=(bn, s // tm, dff // tf),
        in_specs=[pl.BlockSpec((1, tm, d), lambda b, i, f: (b, i, 0)),
                  pl.BlockSpec((1, tm, d), lambda b, i, f: (b, i, 0)),
                  pl.BlockSpec((d, tf), lambda b, i, f: (0, f)),
                  pl.BlockSpec((1, tf), lambda b, i, f: (0, f)),
                  pl.BlockSpec((tf, d), lambda b, i, f: (f, 0)),
                  pl.BlockSpec((1, d), lambda b, i, f: (0, 0)),
                  pl.BlockSpec((1, 6, d), lambda b, i, f: (b, 0, 0))],
        out_specs=pl.BlockSpec((1, tm, d), lambda b, i, f: (b, i, 0)),
        out_shape=jax.ShapeDtypeStruct((bn, s, d), F32),
        scratch_shapes=[pltpu.VMEM((tm, d), F32)],
        compiler_params=_params(("parallel", "parallel", "arbitrary")),
        name="mlp",
    )(h2, x1, w1, b1, w2, b2, mod3)


def _t5_bucket_table():
    n = np.arange(2 * KEY_BLOCK, dtype=np.int64)
    max_exact = N_BUCKETS // 2
    nf = np.maximum(n, 1).astype(np.float32)
    ratio = np.log(nf / np.float32(max_exact)) / np.float32(math.log(MAX_DISTANCE / max_exact))
    large = max_exact + (ratio * np.float32(N_BUCKETS - max_exact)).astype(np.int32)
    large = np.minimum(large, N_BUCKETS - 1)
    bucket = np.where(n < max_exact, n, large)
    assert (bucket[MAX_DISTANCE:] == N_BUCKETS - 1).all()
    return bucket


def _near_bias_table(rel_bias):
    r = np.arange(ATT_QB)[:, None]
    c = np.arange(2 * KEY_BLOCK)[None, :]
    dist = np.clip(KEY_BLOCK + r - c, 0, 2 * KEY_BLOCK - 1)
    idx = _t5_bucket_table()[dist]
    rel_t = rel_bias.astype(F32).T
    return rel_t[:, idx] - rel_t[:, N_BUCKETS - 1][:, None, None]


def _layer(x, mod, norm1_g, norm2_g, w_in, cq_g, w_uq, w_uk, kv_g, qa_g, w_uv, w_iq, ki_g, ki_b, tb,
           conv_w, conv_b, ao_g, co_g, w_out, w_mlp1, b_mlp1, w_mlp2, b_mlp2):
    bn, s, d = x.shape
    topk = min(IDX_TOPK_MAX, s // 4)
    nkb = s // KEY_BLOCK
    mod3 = mod.reshape(bn, 6, d)

    o1 = Q_LORA + KV_LORA
    o2 = o1 + IDX_DIM
    o3 = o2 + IDX_HEADS
    zpad = lambda n: jnp.zeros((d, n), F32)
    w_in_r = jnp.concatenate(
        [w_in[:, :o1], w_in[:, o1:o2], zpad(LANES - IDX_DIM), w_in[:, o2:o3], zpad(LANES - IDX_HEADS),
         w_in[:, o3:]], axis=1).astype(BF16)
    pad_lanes = lambda v: jnp.pad(v, (0, LANES - v.shape[0])).reshape(1, LANES)
    w_iq_pad = jnp.pad(w_iq.reshape(Q_LORA, IDX_HEADS, IDX_DIM),
                       ((0, 0), (0, 0), (0, LANES - IDX_DIM))).reshape(Q_LORA, IDX_HEADS * LANES).astype(BF16)

    tm = min(512, s)
    cq, ckv, kidx, widx, y_conv = _in_call(
        x, mod3, norm1_g.reshape(1, d), w_in_r, cq_g.reshape(1, -1), kv_g.reshape(1, -1),
        pad_lanes(ki_g), pad_lanes(ki_b), conv_w, conv_b.reshape(1, -1), co_g.reshape(1, -1), tm)

    q_abs, iq = _q_call(cq, w_uq.astype(BF16), w_uk.astype(BF16), qa_g.reshape(1, -1), w_iq_pad, tm)

    kidx_t = kidx.reshape(bn, nkb, KEY_BLOCK, LANES).swapaxes(2, 3)
    bias = _idx_call(iq, kidx_t, widx, topk)

    ckv_t = ckv.reshape(bn, nkb, KEY_BLOCK, KV_LORA).swapaxes(2, 3)
    y_attn = _attn_call(q_abs, ckv_t, ckv, bias, tb, w_uv.astype(BF16), ao_g)

    x1, h2 = _out_call(x, y_attn, y_conv, w_out.astype(BF16), mod3, norm2_g.reshape(1, d), tm)
    return _mlp_call(h2, x1, w_mlp1.astype(BF16), b_mlp1.reshape(1, -1), w_mlp2.astype(BF16),
                     b_mlp2.reshape(1, -1), mod3, tm, 1024)


def kernel(x, c, w_ada, b_ada, norm1_g, norm2_g, w_in, cq_norm_g, w_uq, w_uk, kv_norm_g, q_abs_norm_g,
           w_uv, w_iq, idx_k_norm_g, idx_k_norm_b, rel_bias, conv_w, conv_b, attn_out_norm_g,
           conv_out_norm_g, w_out, w_mlp1, b_mlp1, w_mlp2, b_mlp2):
    bn = x.shape[0]
    depth = w_ada.shape[0]
    assert x.shape[1] % (2 * KEY_BLOCK) == 0 and bn <= 8
    tb = _near_bias_table(rel_bias)
    c8 = jnp.pad(c, ((0, 8 - bn), (0, 0)))
    for l in range(depth):
        mod = _mod_call(c8, w_ada[l], b_ada[l].reshape(1, -1))[:bn]
        x = _layer(x, mod, norm1_g[l], norm2_g[l], w_in[l], cq_norm_g[l], w_uq[l], w_uk[l], kv_norm_g[l],
                   q_abs_norm_g[l], w_uv[l], w_iq[l], idx_k_norm_g[l], idx_k_norm_b[l], tb, conv_w[l],
                   conv_b[l], attn_out_norm_g[l], conv_out_norm_g[l], w_out[l], w_mlp1[l], b_mlp1[l],
                   w_mlp2[l], b_mlp2[l])
    return x
```

```python
import functools
import math

import numpy as np
import jax
import jax.numpy as jnp
from jax import lax
from jax.experimental import pallas as pl
from jax.experimental.pallas import tpu as pltpu

F32 = jnp.float32
BF16 = jnp.bfloat16

ATTN_HEADS = 8
HEAD_DIM = 128
Q_LORA = 512
KV_LORA = 256
IDX_HEADS = 16
IDX_DIM = 64
IDX_TOPK_MAX = 256
CONV_WIDTH = 1024
CONV_GROUPS = 8
CONV_GROUP_DIM = 128
CONV_K = 3
N_BUCKETS = 32
MAX_DISTANCE = 128
EPS = 1e-6

LANES = 128
KEY_BLOCK = 256
MASK_NEG = -1e30
M_INIT = -3e38
INT_MIN = -(2 ** 31)
VMEM_LIMIT = 56 * 1024 * 1024

C_CQ = 0
C_KV = C_CQ + Q_LORA
C_KI = C_KV + KV_LORA
C_WI = C_KI + LANES
C_GB = C_WI + LANES
C_GC = C_GB + CONV_WIDTH
C_HC = C_GC + CONV_WIDTH
IN_COLS_PAD = C_HC + CONV_WIDTH


def _dot(a, b):
    return jnp.dot(a, b, preferred_element_type=F32)


def _rms(x, g):
    return x * lax.rsqrt(jnp.mean(x * x, axis=-1, keepdims=True) + EPS) * g


def _params(sem):
    return pltpu.CompilerParams(dimension_semantics=sem, vmem_limit_bytes=VMEM_LIMIT)


def _const_spec(shape):
    n = len(shape)
    return pl.BlockSpec(shape, lambda *_: (0,) * n, pipeline_mode=pl.Buffered(1))


def _mod_kernel(c_ref, w_ref, b_ref, o_ref):
    c = c_ref[...]
    c_act = c / (1.0 + jnp.exp(-c))
    o_ref[...] = _dot(c_act, w_ref[...]) + b_ref[...]


def _mod_call(c8, w_ada, b_ada):
    d, n = w_ada.shape
    tn = 1024
    return pl.pallas_call(
        _mod_kernel,
        grid=(n // tn,),
        in_specs=[pl.BlockSpec((8, d), lambda j: (0, 0)),
                  pl.BlockSpec((d, tn), lambda j: (0, j)),
                  pl.BlockSpec((1, tn), lambda j: (0, j))],
        out_specs=pl.BlockSpec((8, tn), lambda j: (0, j)),
        out_shape=jax.ShapeDtypeStruct((8, n), F32),
        compiler_params=_params(("parallel",)),
        name="mod",
    )(c8, w_ada, b_ada)


def _in_kernel(x_ref, mod_ref, g1_ref, w_ref, cqg_ref, kvg_ref, kig_ref, kib_ref, cw_ref, cb_ref,
               cg_ref, cq_ref, ckv_ref, ki_ref, wi_ref, yc_ref, ubuf_ref, *, tm):
    x = x_ref[0]
    h = _rms(x, g1_ref[...]) * (1.0 + mod_ref[0, 1:2, :]) + mod_ref[0, 0:1, :]
    hb = h.astype(BF16)

    cq_ref[0] = _rms(_dot(hb, w_ref[:, C_CQ:C_CQ + Q_LORA]), cqg_ref[...]).astype(BF16)
    ckv_ref[0] = _rms(_dot(hb, w_ref[:, C_KV:C_KV + KV_LORA]), kvg_ref[...]).astype(BF16)

    ki = _dot(hb, w_ref[:, C_KI:C_KI + LANES])
    real = lax.broadcasted_iota(jnp.int32, (1, LANES), 1) < IDX_DIM
    mu = jnp.sum(jnp.where(real, ki, 0.0), axis=-1, keepdims=True) * (1.0 / IDX_DIM)
    xc = jnp.where(real, ki - mu, 0.0)
    var = jnp.sum(xc * xc, axis=-1, keepdims=True) * (1.0 / IDX_DIM)
    ki_ref[0] = (xc * lax.rsqrt(var + EPS) * kig_ref[...] + kib_ref[...]).astype(BF16)

    wi_ref[0] = _dot(hb, w_ref[:, C_WI:C_WI + LANES]) * (IDX_HEADS ** -0.5 * IDX_DIM ** -0.5)

    @pl.when(pl.program_id(1) == 0)
    def _():
        ubuf_ref[0:8, :] = jnp.zeros((8, CONV_WIDTH), F32)

    half = CONV_WIDTH // 2
    for c0 in (0, half):
        gc = _dot(hb, w_ref[:, C_GC + c0:C_GC + c0 + half])
        hc = _dot(hb, w_ref[:, C_HC + c0:C_HC + c0 + half])
        ubuf_ref[8:8 + tm, c0:c0 + half] = gc * hc
    for c0 in (0, half):
        gb = _dot(hb, w_ref[:, C_GB + c0:C_GB + c0 + half])
        y = cb_ref[:, c0:c0 + half]
        for j in range(CONV_K):
            lo = 8 - (CONV_K - 1) + j
            y = y + cw_ref[j:j + 1, c0:c0 + half] * ubuf_ref[lo:lo + tm, c0:c0 + half]
        y = gb * y
        for g in range(half // CONV_GROUP_DIM):
            l0 = g * CONV_GROUP_DIM
            yg = y[:, l0:l0 + CONV_GROUP_DIM]
            yc_ref[0, :, c0 + l0:c0 + l0 + CONV_GROUP_DIM] = _rms(
                yg, cg_ref[:, c0 + l0:c0 + l0 + CONV_GROUP_DIM]).astype(BF16)
    ubuf_ref[0:8, :] = ubuf_ref[tm:tm + 8, :]


def _in_call(x, mod3, norm1_g, w_in_r, cq_g, kv_g, ki_g, ki_b, conv_w, conv_b, conv_g, tm):
    bn, s, d = x.shape
    kern = functools.partial(_in_kernel, tm=tm)
    tok = lambda w: pl.BlockSpec((1, tm, w), lambda b, i: (b, i, 0))
    return pl.pallas_call(
        kern,
        grid=(bn, s // tm),
        in_specs=[tok(d),
                  pl.BlockSpec((1, 6, d), lambda b, i: (b, 0, 0)),
                  _const_spec((1, d)),
                  _const_spec((d, IN_COLS_PAD)),
                  _const_spec((1, Q_LORA)),
                  _const_spec((1, KV_LORA)),
                  _const_spec((1, LANES)),
                  _const_spec((1, LANES)),
                  _const_spec((CONV_K, CONV_WIDTH)),
                  _const_spec((1, CONV_WIDTH)),
                  _const_spec((1, CONV_WIDTH))],
        out_specs=[tok(Q_LORA), tok(KV_LORA), tok(LANES), tok(LANES), tok(CONV_WIDTH)],
        out_shape=[jax.ShapeDtypeStruct((bn, s, Q_LORA), BF16),
                   jax.ShapeDtypeStruct((bn, s, KV_LORA), BF16),
                   jax.ShapeDtypeStruct((bn, s, LANES), BF16),
                   jax.ShapeDtypeStruct((bn, s, LANES), F32),
                   jax.ShapeDtypeStruct((bn, s, CONV_WIDTH), BF16)],
        scratch_shapes=[pltpu.VMEM((tm + 8, CONV_WIDTH), F32)],
        compiler_params=_params(("parallel", "arbitrary")),
        name="in_proj",
    )(x, mod3, norm1_g, w_in_r, cq_g, kv_g, ki_g, ki_b, conv_w, conv_b, conv_g)


def _q_kernel(cq_ref, wuq_ref, wuk_ref, qg_ref, wiqt_ref, qa_ref, iqt_ref):
    cq = cq_ref[0]
    q = _dot(cq, wuq_ref[...]).astype(BF16)
    for h in range(ATTN_HEADS):
        qa = _dot(q[:, h * HEAD_DIM:(h + 1) * HEAD_DIM], wuk_ref[h])
        qa_ref[0, h] = (_rms(qa, qg_ref[...]) * (KV_LORA ** -0.5)).astype(BF16)
    for h in range(IDX_HEADS):
        iqt_ref[0, h] = lax.dot_general(wiqt_ref[h], cq, (((1,), (1,)), ((), ())),
                                        preferred_element_type=F32).astype(BF16)


def _q_call(cq, w_uq, w_uk, qa_g, w_iq_t, tq):
    bn, s, _ = cq.shape
    return pl.pallas_call(
        _q_kernel,
        grid=(bn, s // tq),
        in_specs=[pl.BlockSpec((1, tq, Q_LORA), lambda b, i: (b, i, 0)),
                  _const_spec((Q_LORA, ATTN_HEADS * HEAD_DIM)),
                  _const_spec((ATTN_HEADS, HEAD_DIM, KV_LORA)),
                  _const_spec((1, KV_LORA)),
                  _const_spec((IDX_HEADS, LANES, Q_LORA))],
        out_specs=[pl.BlockSpec((1, ATTN_HEADS, tq, KV_LORA), lambda b, i: (b, 0, i, 0)),
                   pl.BlockSpec((1, IDX_HEADS, LANES, tq), lambda b, i: (b, 0, 0, i))],
        out_shape=[jax.ShapeDtypeStruct((bn, ATTN_HEADS, s, KV_LORA), BF16),
                   jax.ShapeDtypeStruct((bn, IDX_HEADS, LANES, s), BF16)],
        compiler_params=_params(("parallel", "parallel")),
        name="q_proj",
    )(cq, w_uq, w_uk, qa_g, w_iq_t)


IDX_QB = LANES
SUBLANES = 8


def _idx_kernel(iqt_ref, k_ref, wt_ref, bias_ref, key_ref, *, topk, nkb):
    i = pl.program_id(1)
    nch = (i + 2) // 2
    nv = KEY_BLOCK // SUBLANES
    qpos = i * IDX_QB + lax.broadcasted_iota(jnp.int32, (KEY_BLOCK, IDX_QB), 1)

    def score_chunk(c, carry):
        k0 = pl.multiple_of(c * KEY_BLOCK, KEY_BLOCK)
        k = k_ref[0, pl.ds(k0, KEY_BLOCK), :]
        acc = jnp.zeros((KEY_BLOCK, IDX_QB), F32)
        for hp in range(IDX_HEADS // 2):
            wt = jnp.concatenate([iqt_ref[0, 2 * hp], iqt_ref[0, 2 * hp + 1]], axis=1)
            lg = jnp.maximum(_dot(k, wt), 0.0)
            acc = acc + lg[:, :IDX_QB] * wt_ref[0, 2 * hp:2 * hp + 1, :]
            acc = acc + lg[:, IDX_QB:] * wt_ref[0, 2 * hp + 1:2 * hp + 2, :]
        bits = pltpu.bitcast(acc, jnp.int32)
        key = bits ^ ((bits >> 31) & 0x7FFFFFFF)
        kpos = k0 + lax.broadcasted_iota(jnp.int32, (KEY_BLOCK, IDX_QB), 0)
        key_ref[pl.ds(k0, KEY_BLOCK), :] = jnp.where(kpos <= qpos, key, INT_MIN)
        return carry

    lax.fori_loop(0, nch, score_chunk, 0)

    def count_ge(cand):
        def body(c, cnt):
            k0 = pl.multiple_of(c * KEY_BLOCK, KEY_BLOCK)
            k = key_ref[pl.ds(k0, KEY_BLOCK), :].reshape(nv, SUBLANES, IDX_QB)
            return cnt + jnp.sum(jnp.where(k >= cand[None], 1, 0), axis=0)
        cnt = lax.fori_loop(0, nch, body, jnp.zeros((SUBLANES, IDX_QB), jnp.int32))
        return jnp.sum(cnt.astype(F32), axis=0, keepdims=True)

    zero = jnp.zeros((SUBLANES, IDX_QB), jnp.int32)
    t0 = jnp.where(count_ge(zero) >= topk, zero, INT_MIN)

    def bit_step(j, t):
        cand = t + jnp.left_shift(jnp.int32(1), 30 - j)
        return jnp.where(count_ge(cand) >= topk, cand, t)

    thr = jnp.maximum(lax.fori_loop(0, 31, bit_step, t0), INT_MIN + 1)

    def write_chunk(c, carry):
        k0 = pl.multiple_of(c * KEY_BLOCK, KEY_BLOCK)
        k = key_ref[pl.ds(k0, KEY_BLOCK), :].reshape(nv, SUBLANES, IDX_QB)
        m = jnp.where(k >= thr[None], 0.0, MASK_NEG).reshape(KEY_BLOCK, IDX_QB)
        bias_ref[0, c] = m.T.astype(BF16)
        return carry

    lax.fori_loop(0, nch, write_chunk, 0)

    def fill_chunk(c, carry):
        bias_ref[0, c] = jnp.full((IDX_QB, KEY_BLOCK), MASK_NEG, BF16)
        return carry

    lax.fori_loop(nch, nkb, fill_chunk, 0)


def _idx_call(iq_t, kidx, widx_t, topk):
    bn, s, _ = kidx.shape
    nkb = s // KEY_BLOCK
    kern = functools.partial(_idx_kernel, topk=topk, nkb=nkb)
    return pl.pallas_call(
        kern,
        grid=(bn, s // IDX_QB),
        in_specs=[pl.BlockSpec((1, IDX_HEADS, LANES, IDX_QB), lambda b, i: (b, 0, 0, i)),
                  pl.BlockSpec((1, s, LANES), lambda b, i: (b, 0, 0)),
                  pl.BlockSpec((1, IDX_HEADS, IDX_QB), lambda b, i: (b, 0, i))],
        out_specs=pl.BlockSpec((1, nkb, IDX_QB, KEY_BLOCK), lambda b, i: (b, 0, i, 0)),
        out_shape=jax.ShapeDtypeStruct((bn, nkb, s, KEY_BLOCK), BF16),
        scratch_shapes=[pltpu.VMEM((s, IDX_QB), jnp.int32)],
        compiler_params=_params(("parallel", "arbitrary")),
        name="indexer",
    )(iq_t, kidx, widx_t)


ATT_QB = 256


def _attn_kernel(q_ref, kt_ref, v_ref, bias_ref, tb_ref, wuv_ref, og_ref, out_ref, m_ref, l_ref, acc_ref):
    i = pl.program_id(1)
    rows = ATTN_HEADS * ATT_QB
    nl = KV_LORA // LANES

    m_ref[...] = jnp.full((rows, LANES), M_INIT, F32)
    l_ref[...] = jnp.zeros((rows, LANES), F32)
    acc_ref[...] = jnp.zeros((rows, KV_LORA), F32)

    def block(kb, tb_col):
        kt = kt_ref[0, kb]
        v = v_ref[0, pl.ds(pl.multiple_of(kb * KEY_BLOCK, KEY_BLOCK), KEY_BLOCK), :]
        bias = bias_ref[0, kb].astype(F32)
        for h in range(ATTN_HEADS):
            r0 = h * ATT_QB
            s = _dot(q_ref[0, h], kt) + bias
            if tb_col is not None:
                s = s + tb_ref[h, :, tb_col:tb_col + KEY_BLOCK]
            m_old = m_ref[r0:r0 + ATT_QB, :]
            m_new = jnp.maximum(m_old, jnp.max(s, axis=-1, keepdims=True))
            alpha = jnp.exp(m_old - m_new)
            ps = [jnp.exp(s[:, c * LANES:(c + 1) * LANES] - m_new) for c in range(KEY_BLOCK // LANES)]
            l_ref[r0:r0 + ATT_QB, :] = alpha * l_ref[r0:r0 + ATT_QB, :] + functools.reduce(jnp.add, ps)
            m_ref[r0:r0 + ATT_QB, :] = m_new
            pv = _dot(jnp.concatenate(ps, axis=1).astype(BF16), v)
            acc_ref[r0:r0 + ATT_QB, :] = acc_ref[r0:r0 + ATT_QB, :] * jnp.concatenate([alpha] * nl, axis=1) + pv

    def far(kb, carry):
        block(kb, None)
        return carry

    lax.fori_loop(0, jnp.maximum(i - 1, 0), far, 0)

    @pl.when(i > 0)
    def _():
        block(i - 1, 0)

    block(i, KEY_BLOCK)

    for h in range(ATTN_HEADS):
        r0 = h * ATT_QB
        inv = 1.0 / jnp.sum(l_ref[r0:r0 + ATT_QB, :], axis=-1, keepdims=True)
        o = acc_ref[r0:r0 + ATT_QB, :] * inv
        y = _dot(o.astype(BF16), wuv_ref[h])
        out_ref[0, :, h * HEAD_DIM:(h + 1) * HEAD_DIM] = _rms(y, og_ref[h:h + 1, :]).astype(BF16)


def _attn_call(q_abs, ckv_t, ckv, bias, tb, w_uv, o_g):
    bn, _, s, _ = q_abs.shape
    nkb = s // KEY_BLOCK
    rows = ATTN_HEADS * ATT_QB
    return pl.pallas_call(
        _attn_kernel,
        grid=(bn, s // ATT_QB),
        in_specs=[pl.BlockSpec((1, ATTN_HEADS, ATT_QB, KV_LORA), lambda b, i: (b, 0, i, 0)),
                  pl.BlockSpec((1, nkb, KV_LORA, KEY_BLOCK), lambda b, i: (b, 0, 0, 0)),
                  pl.BlockSpec((1, s, KV_LORA), lambda b, i: (b, 0, 0)),
                  pl.BlockSpec((1, nkb, ATT_QB, KEY_BLOCK), lambda b, i: (b, 0, i, 0)),
                  _const_spec((ATTN_HEADS, ATT_QB, 2 * KEY_BLOCK)),
                  _const_spec((ATTN_HEADS, KV_LORA, HEAD_DIM)),
                  _const_spec((ATTN_HEADS, HEAD_DIM))],
        out_specs=pl.BlockSpec((1, ATT_QB, ATTN_HEADS * HEAD_DIM), lambda b, i: (b, i, 0)),
        out_shape=jax.ShapeDtypeStruct((bn, s, ATTN_HEADS * HEAD_DIM), BF16),
        scratch_shapes=[pltpu.VMEM((rows, LANES), F32),
                        pltpu.VMEM((rows, LANES), F32),
                        pltpu.VMEM((rows, KV_LORA), F32)],
        compiler_params=_params(("parallel", "arbitrary")),
        name="attn",
    )(q_abs, ckv_t, ckv, bias, tb, w_uv, o_g)


def _out_kernel(x_ref, ya_ref, yc_ref, w_ref, mod_ref, g2_ref, x1_ref, h2_ref):
    half = w_ref.shape[0] // 2
    y = _dot(ya_ref[0], w_ref[0:half, :]) + _dot(yc_ref[0], w_ref[half:, :])
    x1 = x_ref[0] + mod_ref[0, 2:3, :] * y
    x1_ref[0] = x1
    h2_ref[0] = (_rms(x1, g2_ref[...]) * (1.0 + mod_ref[0, 4:5, :]) + mod_ref[0, 3:4, :]).astype(BF16)


def _out_call(x, y_attn, y_conv, w_out, mod3, norm2_g, tm):
    bn, s, d = x.shape
    half = y_attn.shape[-1]
    tok = lambda w: pl.BlockSpec((1, tm, w), lambda b, i: (b, i, 0))
    return pl.pallas_call(
        _out_kernel,
        grid=(bn, s // tm),
        in_specs=[tok(d), tok(half), tok(half),
                  _const_spec((2 * half, d)),
                  pl.BlockSpec((1, 6, d), lambda b, i: (b, 0, 0)),
                  _const_spec((1, d))],
        out_specs=[tok(d), tok(d)],
        out_shape=[jax.ShapeDtypeStruct((bn, s, d), F32),
                   jax.ShapeDtypeStruct((bn, s, d), BF16)],
        compiler_params=_params(("parallel", "parallel")),
        name="out_proj",
    )(x, y_attn, y_conv, w_out, mod3, norm2_g)


def _mlp_kernel(h2_ref, x1_ref, w1_ref, b1_ref, w2_ref, b2_ref, mod_ref, o_ref, acc_ref):
    f = pl.program_id(2)
    a = jnp.maximum(_dot(h2_ref[0], w1_ref[...]) + b1_ref[...], 0.0)
    contrib = _dot((a * a).astype(BF16), w2_ref[...])

    @pl.when(f == 0)
    def _():
        acc_ref[...] = contrib

    @pl.when(f > 0)
    def _():
        acc_ref[...] += contrib

    @pl.when(f == pl.num_programs(2) - 1)
    def _():
        o_ref[0] = x1_ref[0] + mod_ref[0, 5:6, :] * (acc_ref[...] + b2_ref[...])


def _mlp_call(h2, x1, w1, b1, w2, b2, mod3, tm, tf):
    bn, s, d = x1.shape
    dff = w1.shape[1]
    return pl.pallas_call(
        _mlp_kernel,
        grid=(bn, s // tm, dff // tf),
        in_specs=[pl.BlockSpec((1, tm, d), lambda b, i, f: (b, i, 0)),
                  pl.BlockSpec((1, tm, d), lambda b, i, f: (b, i, 0)),
                  pl.BlockSpec((d, tf), lambda b, i, f: (0, f)),
                  pl.BlockSpec((1, tf), lambda b, i, f: (0, f)),
                  pl.BlockSpec((tf, d), lambda b, i, f: (f, 0)),
                  pl.BlockSpec((1, d), lambda b, i, f: (0, 0)),
                  pl.BlockSpec((1, 6, d), lambda b, i, f: (b, 0, 0))],
        out_specs=pl.BlockSpec((1, tm, d), lambda b, i, f: (b, i, 0)),
        out_shape=jax.ShapeDtypeStruct((bn, s, d), F32),
        scratch_shapes=[pltpu.VMEM((tm, d), F32)],
        compiler_params=_params(("parallel", "parallel", "arbitrary")),
        name="mlp",
    )(h2, x1, w1, b1, w2, b2, mod3)


def _t5_bucket_table():
    n = np.arange(2 * KEY_BLOCK, dtype=np.int64)
    max_exact = N_BUCKETS // 2
    nf = np.maximum(n, 1).astype(np.float32)
    ratio = np.log(nf / np.float32(max_exact)) / np.float32(math.log(MAX_DISTANCE / max_exact))
    large = max_exact + (ratio * np.float32(N_BUCKETS - max_exact)).astype(np.int32)
    large = np.minimum(large, N_BUCKETS - 1)
    bucket = np.where(n < max_exact, n, large)
    assert (bucket[MAX_DISTANCE:] == N_BUCKETS - 1).all()
    return bucket


def _near_bias_table(rel_bias):
    r = np.arange(ATT_QB)[:, None]
    c = np.arange(2 * KEY_BLOCK)[None, :]
    dist = np.clip(KEY_BLOCK + r - c, 0, 2 * KEY_BLOCK - 1)
    idx = jnp.asarray(_t5_bucket_table()[dist])
    rel_t = rel_bias.astype(F32).T
    tb = jnp.zeros((rel_t.shape[0],) + idx.shape, F32)
    for b in range(N_BUCKETS - 1):
        tb = jnp.where(idx[None] == b, (rel_t[:, b] - rel_t[:, N_BUCKETS - 1])[:, None, None], tb)
    return tb


def _layer(x, mod, norm1_g, norm2_g, w_in, cq_g, w_uq, w_uk, kv_g, qa_g, w_uv, w_iq, ki_g, ki_b, tb,
           conv_w, conv_b, ao_g, co_g, w_out, w_mlp1, b_mlp1, w_mlp2, b_mlp2):
    bn, s, d = x.shape
    topk = min(IDX_TOPK_MAX, s // 4)
    nkb = s // KEY_BLOCK
    mod3 = mod.reshape(bn, 6, d)

    o1 = Q_LORA + KV_LORA
    o2 = o1 + IDX_DIM
    o3 = o2 + IDX_HEADS
    zpad = lambda n: jnp.zeros((d, n), F32)
    w_in_r = jnp.concatenate(
        [w_in[:, :o1], w_in[:, o1:o2], zpad(LANES - IDX_DIM), w_in[:, o2:o3], zpad(LANES - IDX_HEADS),
         w_in[:, o3:]], axis=1).astype(BF16)
    pad_lanes = lambda v: jnp.pad(v, (0, LANES - v.shape[0])).reshape(1, LANES)
    w_iq_t = jnp.pad(w_iq.reshape(Q_LORA, IDX_HEADS, IDX_DIM).transpose(1, 2, 0),
                     ((0, 0), (0, LANES - IDX_DIM), (0, 0))).astype(BF16)

    tm = min(512, s)
    cq, ckv, kidx, widx, y_conv = _in_call(
        x, mod3, norm1_g.reshape(1, d), w_in_r, cq_g.reshape(1, -1), kv_g.reshape(1, -1),
        pad_lanes(ki_g), pad_lanes(ki_b), conv_w, conv_b.reshape(1, -1), co_g.reshape(1, -1), tm)

    q_abs, iq_t = _q_call(cq, w_uq.astype(BF16), w_uk.astype(BF16), qa_g.reshape(1, -1), w_iq_t, tm)

    widx_t = widx[:, :, :IDX_HEADS].swapaxes(1, 2)
    bias = _idx_call(iq_t, kidx, widx_t, topk)

    ckv_t = ckv.reshape(bn, nkb, KEY_BLOCK, KV_LORA).swapaxes(2, 3)
    y_attn = _attn_call(q_abs, ckv_t, ckv, bias, tb, w_uv.astype(BF16), ao_g)

    x1, h2 = _out_call(x, y_attn, y_conv, w_out.astype(BF16), mod3, norm2_g.reshape(1, d), tm)
    return _mlp_call(h2, x1, w_mlp1.astype(BF16), b_mlp1.reshape(1, -1), w_mlp2.astype(BF16),
                     b_mlp2.reshape(1, -1), mod3, tm, 1024)


def kernel(x, c, w_ada, b_ada, norm1_g, norm2_g, w_in, cq_norm_g, w_uq, w_uk, kv_norm_g, q_abs_norm_g,
           w_uv, w_iq, idx_k_norm_g, idx_k_norm_b, rel_bias, conv_w, conv_b, attn_out_norm_g,
           conv_out_norm_g, w_out, w_mlp1, b_mlp1, w_mlp2, b_mlp2):
    bn = x.shape[0]
    depth = w_ada.shape[0]
    assert x.shape[1] % (2 * KEY_BLOCK) == 0 and bn <= 8
    tb = _near_bias_table(rel_bias)
    c8 = jnp.pad(c, ((0, 8 - bn), (0, 0)))
    for l in range(depth):
        mod = _mod_call(c8, w_ada[l], b_ada[l].reshape(1, -1))[:bn]
        x = _layer(x, mod, norm1_g[l], norm2_g[l], w_in[l], cq_norm_g[l], w_uq[l], w_uk[l], kv_norm_g[l],
                   q_abs_norm_g[l], w_uv[l], w_iq[l], idx_k_norm_g[l], idx_k_norm_b[l], tb, conv_w[l],
                   conv_b[l], attn_out_norm_g[l], conv_out_norm_g[l], w_out[l], w_mlp1[l], b_mlp1[l],
                   w_mlp2[l], b_mlp2[l])
    return x
```

```python
import functools
import math

import numpy as np
import jax
import jax.numpy as jnp
from jax import lax
from jax.experimental import pallas as pl
from jax.experimental.pallas import tpu as pltpu

F32 = jnp.float32
BF16 = jnp.bfloat16

ATTN_HEADS = 8
HEAD_DIM = 128
Q_LORA = 512
KV_LORA = 256
IDX_HEADS = 16
IDX_DIM = 64
IDX_TOPK_MAX = 256
CONV_WIDTH = 1024
CONV_GROUPS = 8
CONV_GROUP_DIM = 128
CONV_K = 3
N_BUCKETS = 32
MAX_DISTANCE = 128
EPS = 1e-6
LOG2E = math.log2(math.e)

LANES = 128
KEY_BLOCK = 256
MASK_NEG = -1e30
M_INIT = -3e38
INT_MIN = -(2 ** 31)
VMEM_LIMIT = 56 * 1024 * 1024

C_CQ = 0
C_KV = C_CQ + Q_LORA
C_KI = C_KV + KV_LORA
C_WI = C_KI + LANES
C_GB = C_WI + LANES
C_GC = C_GB + CONV_WIDTH
C_HC = C_GC + CONV_WIDTH
IN_COLS_PAD = C_HC + CONV_WIDTH


def _dot(a, b):
    return jnp.dot(a, b, preferred_element_type=F32)


def _rms(x, g):
    return x * lax.rsqrt(jnp.mean(x * x, axis=-1, keepdims=True) + EPS) * g


def _params(sem):
    return pltpu.CompilerParams(dimension_semantics=sem, vmem_limit_bytes=VMEM_LIMIT)


def _const_spec(shape):
    n = len(shape)
    return pl.BlockSpec(shape, lambda *_: (0,) * n, pipeline_mode=pl.Buffered(1))


def _mod_kernel(c_ref, w_ref, b_ref, o_ref):
    c = c_ref[...]
    c_act = c / (1.0 + jnp.exp(-c))
    o_ref[...] = _dot(c_act, w_ref[...]) + b_ref[...]


def _mod_call(c8, w_ada, b_ada):
    d, n = w_ada.shape
    tn = 1024
    return pl.pallas_call(
        _mod_kernel,
        grid=(n // tn,),
        in_specs=[pl.BlockSpec((8, d), lambda j: (0, 0)),
                  pl.BlockSpec((d, tn), lambda j: (0, j)),
                  pl.BlockSpec((1, tn), lambda j: (0, j))],
        out_specs=pl.BlockSpec((8, tn), lambda j: (0, j)),
        out_shape=jax.ShapeDtypeStruct((8, n), F32),
        compiler_params=_params(("parallel",)),
        name="mod",
    )(c8, w_ada, b_ada)


def _in_kernel(x_ref, mod_ref, g1_ref, w_ref, cqg_ref, kvg_ref, kig_ref, kib_ref, cw_ref, cb_ref,
               cg_ref, cq_ref, ckv_ref, ki_ref, wi_ref, yc_ref, ubuf_ref, *, tm):
    x = x_ref[0]
    h = _rms(x, g1_ref[...]) * (1.0 + mod_ref[0, 1:2, :]) + mod_ref[0, 0:1, :]
    hb = h.astype(BF16)

    cq_ref[0] = _rms(_dot(hb, w_ref[:, C_CQ:C_CQ + Q_LORA]), cqg_ref[...]).astype(BF16)
    ckv_ref[0] = _rms(_dot(hb, w_ref[:, C_KV:C_KV + KV_LORA]), kvg_ref[...]).astype(BF16)

    ki = _dot(hb, w_ref[:, C_KI:C_KI + LANES])
    real = lax.broadcasted_iota(jnp.int32, (1, LANES), 1) < IDX_DIM
    mu = jnp.sum(jnp.where(real, ki, 0.0), axis=-1, keepdims=True) * (1.0 / IDX_DIM)
    xc = jnp.where(real, ki - mu, 0.0)
    var = jnp.sum(xc * xc, axis=-1, keepdims=True) * (1.0 / IDX_DIM)
    ki_ref[0] = (xc * lax.rsqrt(var + EPS) * kig_ref[...] + kib_ref[...]).astype(BF16)

    wi_ref[0] = _dot(hb, w_ref[:, C_WI:C_WI + LANES]) * (IDX_HEADS ** -0.5 * IDX_DIM ** -0.5)

    @pl.when(pl.program_id(1) == 0)
    def _():
        ubuf_ref[0:8, :] = jnp.zeros((8, CONV_WIDTH), F32)

    half = CONV_WIDTH // 2
    for c0 in (0, half):
        gc = _dot(hb, w_ref[:, C_GC + c0:C_GC + c0 + half])
        hc = _dot(hb, w_ref[:, C_HC + c0:C_HC + c0 + half])
        ubuf_ref[8:8 + tm, c0:c0 + half] = gc * hc
    for c0 in (0, half):
        gb = _dot(hb, w_ref[:, C_GB + c0:C_GB + c0 + half])
        y = cb_ref[:, c0:c0 + half]
        for j in range(CONV_K):
            lo = 8 - (CONV_K - 1) + j
            y = y + cw_ref[j:j + 1, c0:c0 + half] * ubuf_ref[lo:lo + tm, c0:c0 + half]
        y = gb * y
        for g in range(half // CONV_GROUP_DIM):
            l0 = g * CONV_GROUP_DIM
            yg = y[:, l0:l0 + CONV_GROUP_DIM]
            yc_ref[0, :, c0 + l0:c0 + l0 + CONV_GROUP_DIM] = _rms(
                yg, cg_ref[:, c0 + l0:c0 + l0 + CONV_GROUP_DIM]).astype(BF16)
    ubuf_ref[0:8, :] = ubuf_ref[tm:tm + 8, :]


def _in_call(x, mod3, norm1_g, w_in_r, cq_g, kv_g, ki_g, ki_b, conv_w, conv_b, conv_g, tm):
    bn, s, d = x.shape
    kern = functools.partial(_in_kernel, tm=tm)
    tok = lambda w: pl.BlockSpec((1, tm, w), lambda b, i: (b, i, 0))
    return pl.pallas_call(
        kern,
        grid=(bn, s // tm),
        in_specs=[tok(d),
                  pl.BlockSpec((1, 6, d), lambda b, i: (b, 0, 0)),
                  _const_spec((1, d)),
                  _const_spec((d, IN_COLS_PAD)),
                  _const_spec((1, Q_LORA)),
                  _const_spec((1, KV_LORA)),
                  _const_spec((1, LANES)),
                  _const_spec((1, LANES)),
                  _const_spec((CONV_K, CONV_WIDTH)),
                  _const_spec((1, CONV_WIDTH)),
                  _const_spec((1, CONV_WIDTH))],
        out_specs=[tok(Q_LORA), tok(KV_LORA), tok(LANES), tok(LANES), tok(CONV_WIDTH)],
        out_shape=[jax.ShapeDtypeStruct((bn, s, Q_LORA), BF16),
                   jax.ShapeDtypeStruct((bn, s, KV_LORA), BF16),
                   jax.ShapeDtypeStruct((bn, s, LANES), BF16),
                   jax.ShapeDtypeStruct((bn, s, LANES), F32),
                   jax.ShapeDtypeStruct((bn, s, CONV_WIDTH), BF16)],
        scratch_shapes=[pltpu.VMEM((tm + 8, CONV_WIDTH), F32)],
        compiler_params=_params(("parallel", "arbitrary")),
        name="in_proj",
    )(x, mod3, norm1_g, w_in_r, cq_g, kv_g, ki_g, ki_b, conv_w, conv_b, conv_g)


def _q_kernel(cq_ref, wuq_ref, wuk_ref, qg_ref, wiqt_ref, qa_ref, iqt_ref):
    cq = cq_ref[0]
    q = _dot(cq, wuq_ref[...]).astype(BF16)
    for h in range(ATTN_HEADS):
        qa = _dot(q[:, h * HEAD_DIM:(h + 1) * HEAD_DIM], wuk_ref[h])
        qa_ref[0, h] = (_rms(qa, qg_ref[...]) * (KV_LORA ** -0.5 * LOG2E)).astype(BF16)
    for h in range(IDX_HEADS):
        iqt_ref[0, h] = lax.dot_general(wiqt_ref[h], cq, (((1,), (1,)), ((), ())),
                                        preferred_element_type=F32).astype(BF16)


def _q_call(cq, w_uq, w_uk, qa_g, w_iq_t, tq):
    bn, s, _ = cq.shape
    return pl.pallas_call(
        _q_kernel,
        grid=(bn, s // tq),
        in_specs=[pl.BlockSpec((1, tq, Q_LORA), lambda b, i: (b, i, 0)),
                  _const_spec((Q_LORA, ATTN_HEADS * HEAD_DIM)),
                  _const_spec((ATTN_HEADS, HEAD_DIM, KV_LORA)),
                  _const_spec((1, KV_LORA)),
                  _const_spec((IDX_HEADS, LANES, Q_LORA))],
        out_specs=[pl.BlockSpec((1, ATTN_HEADS, tq, KV_LORA), lambda b, i: (b, 0, i, 0)),
                   pl.BlockSpec((1, IDX_HEADS, LANES, tq), lambda b, i: (b, 0, 0, i))],
        out_shape=[jax.ShapeDtypeStruct((bn, ATTN_HEADS, s, KV_LORA), BF16),
                   jax.ShapeDtypeStruct((bn, IDX_HEADS, LANES, s), BF16)],
        compiler_params=_params(("parallel", "parallel")),
        name="q_proj",
    )(cq, w_uq, w_uk, qa_g, w_iq_t)


IDX_QB = LANES
SUBLANES = 8
PACK_ROWS = 16
HALF_SPAN = 2 ** 15
BF16_EXACT_INT = 256


def _idx_kernel(iqt_ref, k_ref, wt_ref, bias_ref, key_ref, hi_ref, lo_ref, *, topk, nkb):
    i = pl.program_id(1)
    nch = (i + 2) // 2
    nsu = (nch + 1) // 2
    nv = KEY_BLOCK // SUBLANES
    unit = 2 * KEY_BLOCK
    nvp = unit // PACK_ROWS
    qpos = i * IDX_QB + lax.broadcasted_iota(jnp.int32, (unit, IDX_QB), 1)

    def score_unit(u, carry):
        k0 = pl.multiple_of(u * unit, unit)
        k = k_ref[0, pl.ds(k0, unit), :]
        acc = jnp.zeros((unit, IDX_QB), F32)
        for hp in range(IDX_HEADS // 2):
            wt = jnp.concatenate([iqt_ref[0, 2 * hp], iqt_ref[0, 2 * hp + 1]], axis=1)
            lg = jnp.maximum(_dot(k, wt), 0.0)
            acc = acc + lg[:, :IDX_QB] * wt_ref[0, 2 * hp:2 * hp + 1, :]
            acc = acc + lg[:, IDX_QB:] * wt_ref[0, 2 * hp + 1:2 * hp + 2, :]
        bits = pltpu.bitcast(acc, jnp.int32)
        key = bits ^ ((bits >> 31) & 0x7FFFFFFF)
        kpos = k0 + lax.broadcasted_iota(jnp.int32, (unit, IDX_QB), 0)
        key = jnp.where(kpos <= qpos, key, INT_MIN)
        key_ref[pl.ds(k0, unit), :] = key
        hi_ref[pl.ds(k0, unit), :] = (key >> 16).astype(jnp.int16)
        lo_ref[pl.ds(k0, unit), :] = ((key & 0xFFFF) - HALF_SPAN).astype(jnp.int16)
        return carry

    lax.fori_loop(0, nsu, score_unit, 0)


    def count(ref, cand16, cmp):
        def body(u, cnt):
            v = ref[pl.ds(pl.multiple_of(u * unit, unit), unit), :].reshape(nvp, PACK_ROWS, IDX_QB)
            ones = jnp.where(cmp(v, cand16[None]), jnp.bfloat16(1), jnp.bfloat16(0))
            parts = [ones[j] for j in range(nvp)]
            while len(parts) > 1:
                parts = [parts[a] + parts[a + 1] for a in range(0, len(parts), 2)]
            return cnt + parts[0]
        cnt = lax.fori_loop(0, nsu, body, jnp.zeros((PACK_ROWS, IDX_QB), BF16))
        return jnp.sum(cnt.astype(F32), axis=0, keepdims=True)

    def bisect16(ref, need):
        def step(j, t):
            cand = t + jnp.left_shift(jnp.int32(1), 15 - j)
            ok = count(ref, cand.astype(jnp.int16), lambda a, b: a >= b) >= need
            return jnp.where(ok, cand, t)
        return lax.fori_loop(0, 16, step, jnp.full((PACK_ROWS, IDX_QB), -HALF_SPAN, jnp.int32))

    p_hi = bisect16(hi_ref, float(topk))
    need_lo = float(topk) - count(hi_ref, p_hi.astype(jnp.int16), lambda a, b: a > b)
    p_hi16 = p_hi.astype(jnp.int16)

    def restrict(u, carry):
        sl = pl.ds(pl.multiple_of(u * unit, unit), unit)
        h = hi_ref[sl, :].reshape(nvp, PACK_ROWS, IDX_QB)
        l = lo_ref[sl, :].reshape(nvp, PACK_ROWS, IDX_QB)
        lo_ref[sl, :] = jnp.where(h == p_hi16[None], l, jnp.int16(-HALF_SPAN)).reshape(unit, IDX_QB)
        return carry

    lax.fori_loop(0, nsu, restrict, 0)
    p_lo = bisect16(lo_ref, need_lo)
    thr = jnp.maximum(p_hi * (2 * HALF_SPAN) + (p_lo + HALF_SPAN), INT_MIN + 1)[:SUBLANES]

    def write_chunk(c, carry):
        k0 = pl.multiple_of(c * KEY_BLOCK, KEY_BLOCK)
        k = key_ref[pl.ds(k0, KEY_BLOCK), :].reshape(nv, SUBLANES, IDX_QB)
        m = jnp.where(k >= thr[None], 0.0, MASK_NEG).reshape(KEY_BLOCK, IDX_QB)
        bias_ref[0, c] = m.T.astype(BF16)
        return carry

    lax.fori_loop(0, nch, write_chunk, 0)

    def fill_chunk(c, carry):
        bias_ref[0, c] = jnp.full((IDX_QB, KEY_BLOCK), MASK_NEG, BF16)
        return carry

    lax.fori_loop(nch, nkb, fill_chunk, 0)


def _idx_call(iq_t, kidx, widx_t, topk):
    bn, s, _ = kidx.shape
    nkb = s // KEY_BLOCK
    kern = functools.partial(_idx_kernel, topk=topk, nkb=nkb)
    return pl.pallas_call(
        kern,
        grid=(bn, s // IDX_QB),
        in_specs=[pl.BlockSpec((1, IDX_HEADS, LANES, IDX_QB), lambda b, i: (b, 0, 0, i)),
                  pl.BlockSpec((1, s, LANES), lambda b, i: (b, 0, 0)),
                  pl.BlockSpec((1, IDX_HEADS, IDX_QB), lambda b, i: (b, 0, i))],
        out_specs=pl.BlockSpec((1, nkb, IDX_QB, KEY_BLOCK), lambda b, i: (b, 0, i, 0)),
        out_shape=jax.ShapeDtypeStruct((bn, nkb, s, KEY_BLOCK), BF16),
        scratch_shapes=[pltpu.VMEM((s, IDX_QB), jnp.int32),
                        pltpu.VMEM((s, IDX_QB), jnp.int16),
                        pltpu.VMEM((s, IDX_QB), jnp.int16)],
        compiler_params=_params(("parallel", "arbitrary")),
        name="indexer",
    )(iq_t, kidx, widx_t)


ATT_QB = 256


def _attn_kernel(q_ref, kt_ref, v_ref, bias_ref, tb_ref, wuv_ref, og_ref, out_ref, m_ref, l_ref, acc_ref):
    i = pl.program_id(1)
    rows = ATTN_HEADS * ATT_QB
    nl = KV_LORA // LANES

    m_ref[...] = jnp.full((rows, LANES), M_INIT, F32)
    l_ref[...] = jnp.zeros((rows, LANES), F32)
    acc_ref[...] = jnp.zeros((rows, KV_LORA), F32)

    def block(kb, nb, tb_col):
        width = nb * KEY_BLOCK
        kt = jnp.concatenate([kt_ref[0, kb + j] for j in range(nb)], axis=1)
        v = v_ref[0, pl.ds(pl.multiple_of(kb * KEY_BLOCK, KEY_BLOCK), width), :]
        bias = jnp.concatenate([bias_ref[0, kb + j] for j in range(nb)], axis=1).astype(F32)
        for h in range(ATTN_HEADS):
            r0 = h * ATT_QB
            s = _dot(q_ref[0, h], kt) + bias
            if tb_col is not None:
                s = s + tb_ref[h, :, tb_col:tb_col + width]
            m_old = m_ref[r0:r0 + ATT_QB, :]
            m_new = jnp.maximum(m_old, jnp.max(s, axis=-1, keepdims=True))
            alpha = jnp.exp2(m_old - m_new)
            ps = [jnp.exp2(s[:, c * LANES:(c + 1) * LANES] - m_new) for c in range(width // LANES)]
            l_ref[r0:r0 + ATT_QB, :] = alpha * l_ref[r0:r0 + ATT_QB, :] + functools.reduce(jnp.add, ps)
            m_ref[r0:r0 + ATT_QB, :] = m_new
            pv = _dot(jnp.concatenate(ps, axis=1).astype(BF16), v)
            acc_ref[r0:r0 + ATT_QB, :] = acc_ref[r0:r0 + ATT_QB, :] * jnp.concatenate([alpha] * nl, axis=1) + pv

    nfar = jnp.maximum(i - 1, 0)

    def far(j, carry):
        block(2 * j, 2, None)
        return carry

    lax.fori_loop(0, nfar // 2, far, 0)

    @pl.when(nfar % 2 == 1)
    def _():
        block(nfar - 1, 1, None)

    @pl.when(i > 0)
    def _():
        block(i - 1, 2, 0)

    @pl.when(i == 0)
    def _():
        block(0, 1, KEY_BLOCK)

    for h in range(ATTN_HEADS):
        r0 = h * ATT_QB
        inv = 1.0 / jnp.sum(l_ref[r0:r0 + ATT_QB, :], axis=-1, keepdims=True)
        o = acc_ref[r0:r0 + ATT_QB, :] * inv
        y = _dot(o.astype(BF16), wuv_ref[h])
        out_ref[0, :, h * HEAD_DIM:(h + 1) * HEAD_DIM] = _rms(y, og_ref[h:h + 1, :]).astype(BF16)


def _attn_call(q_abs, ckv_t, ckv, bias, tb, w_uv, o_g):
    bn, _, s, _ = q_abs.shape
    nkb = s // KEY_BLOCK
    rows = ATTN_HEADS * ATT_QB
    return pl.pallas_call(
        _attn_kernel,
        grid=(bn, s // ATT_QB),
        in_specs=[pl.BlockSpec((1, ATTN_HEADS, ATT_QB, KV_LORA), lambda b, i: (b, 0, i, 0)),
                  pl.BlockSpec((1, nkb, KV_LORA, KEY_BLOCK), lambda b, i: (b, 0, 0, 0)),
                  pl.BlockSpec((1, s, KV_LORA), lambda b, i: (b, 0, 0)),
                  pl.BlockSpec((1, nkb, ATT_QB, KEY_BLOCK), lambda b, i: (b, 0, i, 0)),
                  _const_spec((ATTN_HEADS, ATT_QB, 2 * KEY_BLOCK)),
                  _const_spec((ATTN_HEADS, KV_LORA, HEAD_DIM)),
                  _const_spec((ATTN_HEADS, HEAD_DIM))],
        out_specs=pl.BlockSpec((1, ATT_QB, ATTN_HEADS * HEAD_DIM), lambda b, i: (b, i, 0)),
        out_shape=jax.ShapeDtypeStruct((bn, s, ATTN_HEADS * HEAD_DIM), BF16),
        scratch_shapes=[pltpu.VMEM((rows, LANES), F32),
                        pltpu.VMEM((rows, LANES), F32),
                        pltpu.VMEM((rows, KV_LORA), F32)],
        compiler_params=_params(("parallel", "arbitrary")),
        name="attn",
    )(q_abs, ckv_t, ckv, bias, tb, w_uv, o_g)


def _out_kernel(x_ref, ya_ref, yc_ref, w_ref, mod_ref, g2_ref, x1_ref, h2_ref):
    half = w_ref.shape[0] // 2
    y = _dot(ya_ref[0], w_ref[0:half, :]) + _dot(yc_ref[0], w_ref[half:, :])
    x1 = x_ref[0] + mod_ref[0, 2:3, :] * y
    x1_ref[0] = x1
    h2_ref[0] = (_rms(x1, g2_ref[...]) * (1.0 + mod_ref[0, 4:5, :]) + mod_ref[0, 3:4, :]).astype(BF16)


def _out_call(x, y_attn, y_conv, w_out, mod3, norm2_g, tm):
    bn, s, d = x.shape
    half = y_attn.shape[-1]
    tok = lambda w: pl.BlockSpec((1, tm, w), lambda b, i: (b, i, 0))
    return pl.pallas_call(
        _out_kernel,
        grid=(bn, s // tm),
        in_specs=[tok(d), tok(half), tok(half),
                  _const_spec((2 * half, d)),
                  pl.BlockSpec((1, 6, d), lambda b, i: (b, 0, 0)),
                  _const_spec((1, d))],
        out_specs=[tok(d), tok(d)],
        out_shape=[jax.ShapeDtypeStruct((bn, s, d), F32),
                   jax.ShapeDtypeStruct((bn, s, d), BF16)],
        compiler_params=_params(("parallel", "parallel")),
        name="out_proj",
    )(x, y_attn, y_conv, w_out, mod3, norm2_g)


MLP_SUBCHUNKS = 2


def _mlp_kernel(h2_ref, x1_ref, w1_ref, b1_ref, w2_ref, b2_ref, mod_ref, o_ref, acc_ref):
    f = pl.program_id(2)

    @pl.when(f == 0)
    def _():
        acc_ref[...] = jnp.zeros(acc_ref.shape, F32)

    sub = w1_ref.shape[1] // MLP_SUBCHUNKS
    contrib = None
    for j in range(MLP_SUBCHUNKS):
        a = jnp.maximum(_dot(h2_ref[0], w1_ref[:, j * sub:(j + 1) * sub]) + b1_ref[:, j * sub:(j + 1) * sub], 0.0)
        part = _dot((a * a).astype(BF16), w2_ref[j * sub:(j + 1) * sub, :])
        contrib = part if contrib is None else contrib + part
    acc_ref[...] += contrib

    @pl.when(f == pl.num_programs(2) - 1)
    def _():
        o_ref[0] = x1_ref[0] + mod_ref[0, 5:6, :] * (acc_ref[...] + b2_ref[...])


def _mlp_call(h2, x1, w1, b1, w2, b2, mod3, tm, tf):
    bn, s, d = x1.shape
    dff = w1.shape[1]
    return pl.pallas_call(
        _mlp_kernel,
        grid=(bn, s // tm, dff // tf),
        in_specs=[pl.BlockSpec((1, tm, d), lambda b, i, f: (b, i, 0)),
                  pl.BlockSpec((1, tm, d), lambda b, i, f: (b, i, 0)),
                  pl.BlockSpec((d, tf), lambda b, i, f: (0, f)),
                  pl.BlockSpec((1, tf), lambda b, i, f: (0, f)),
                  pl.BlockSpec((tf, d), lambda b, i, f: (f, 0)),
                  pl.BlockSpec((1, d), lambda b, i, f: (0, 0)),
                  pl.BlockSpec((1, 6, d), lambda b, i, f: (b, 0, 0))],
        out_specs=pl.BlockSpec((1, tm, d), lambda b, i, f: (b, i, 0)),
        out_shape=jax.ShapeDtypeStruct((bn, s, d), F32),
        scratch_shapes=[pltpu.VMEM((tm, d), F32)],
        compiler_params=_params(("parallel", "parallel", "arbitrary")),
        name="mlp",
    )(h2, x1, w1, b1, w2, b2, mod3)


def _t5_bucket_table():
    n = np.arange(2 * KEY_BLOCK, dtype=np.int64)
    max_exact = N_BUCKETS // 2
    nf = np.maximum(n, 1).astype(np.float32)
    ratio = np.log(nf / np.float32(max_exact)) / np.float32(math.log(MAX_DISTANCE / max_exact))
    large = max_exact + (ratio * np.float32(N_BUCKETS - max_exact)).astype(np.int32)
    large = np.minimum(large, N_BUCKETS - 1)
    bucket = np.where(n < max_exact, n, large)
    assert (bucket[MAX_DISTANCE:] == N_BUCKETS - 1).all()
    return bucket


def _near_bias_table(rel_bias):
    r = np.arange(ATT_QB)[:, None]
    c = np.arange(2 * KEY_BLOCK)[None, :]
    dist = np.clip(KEY_BLOCK + r - c, 0, 2 * KEY_BLOCK - 1)
    idx = jnp.asarray(_t5_bucket_table()[dist])
    rel_t = rel_bias.astype(F32).T * LOG2E
    tb = jnp.zeros((rel_t.shape[0],) + idx.shape, F32)
    for b in range(N_BUCKETS - 1):
        tb = jnp.where(idx[None] == b, (rel_t[:, b] - rel_t[:, N_BUCKETS - 1])[:, None, None], tb)
    return tb


def _layer(x, mod, norm1_g, norm2_g, w_in, cq_g, w_uq, w_uk, kv_g, qa_g, w_uv, w_iq, ki_g, ki_b, tb,
           conv_w, conv_b, ao_g, co_g, w_out, w_mlp1, b_mlp1, w_mlp2, b_mlp2):
    bn, s, d = x.shape
    topk = min(IDX_TOPK_MAX, s // 4)
    nkb = s // KEY_BLOCK
    mod3 = mod.reshape(bn, 6, d)

    o1 = Q_LORA + KV_LORA
    o2 = o1 + IDX_DIM
    o3 = o2 + IDX_HEADS
    zpad = lambda n: jnp.zeros((d, n), F32)
    w_in_r = jnp.concatenate(
        [w_in[:, :o1], w_in[:, o1:o2], zpad(LANES - IDX_DIM), w_in[:, o2:o3], zpad(LANES - IDX_HEADS),
         w_in[:, o3:]], axis=1).astype(BF16)
    pad_lanes = lambda v: jnp.pad(v, (0, LANES - v.shape[0])).reshape(1, LANES)
    w_iq_t = jnp.pad(w_iq.reshape(Q_LORA, IDX_HEADS, IDX_DIM).transpose(1, 2, 0),
                     ((0, 0), (0, LANES - IDX_DIM), (0, 0))).astype(BF16)

    tm = min(512, s)
    cq, ckv, kidx, widx, y_conv = _in_call(
        x, mod3, norm1_g.reshape(1, d), w_in_r, cq_g.reshape(1, -1), kv_g.reshape(1, -1),
        pad_lanes(ki_g), pad_lanes(ki_b), conv_w, conv_b.reshape(1, -1), co_g.reshape(1, -1), tm)

    q_abs, iq_t = _q_call(cq, w_uq.astype(BF16), w_uk.astype(BF16), qa_g.reshape(1, -1), w_iq_t, tm)

    widx_t = widx[:, :, :IDX_HEADS].swapaxes(1, 2)
    bias = _idx_call(iq_t, kidx, widx_t, topk)

    ckv_t = ckv.reshape(bn, nkb, KEY_BLOCK, KV_LORA).swapaxes(2, 3)
    y_attn = _attn_call(q_abs, ckv_t, ckv, bias, tb, w_uv.astype(BF16), ao_g)

    x1, h2 = _out_call(x, y_attn, y_conv, w_out.astype(BF16), mod3, norm2_g.reshape(1, d), tm)
    return _mlp_call(h2, x1, w_mlp1.astype(BF16), b_mlp1.reshape(1, -1), w_mlp2.astype(BF16),
                     b_mlp2.reshape(1, -1), mod3, tm, 1024)


def kernel(x, c, w_ada, b_ada, norm1_g, norm2_g, w_in, cq_norm_g, w_uq, w_uk, kv_norm_g, q_abs_norm_g,
           w_uv, w_iq, idx_k_norm_g, idx_k_norm_b, rel_bias, conv_w, conv_b, attn_out_norm_g,
           conv_out_norm_g, w_out, w_mlp1, b_mlp1, w_mlp2, b_mlp2):
    bn = x.shape[0]
    depth = w_ada.shape[0]
    assert x.shape[1] % (2 * KEY_BLOCK) == 0 and bn <= 8
    assert x.shape[1] // PACK_ROWS <= BF16_EXACT_INT
    tb = _near_bias_table(rel_bias)
    c8 = jnp.pad(c, ((0, 8 - bn), (0, 0)))
    for l in range(depth):
        mod = _mod_call(c8, w_ada[l], b_ada[l].reshape(1, -1))[:bn]
        x = _layer(x, mod, norm1_g[l], norm2_g[l], w_in[l], cq_norm_g[l], w_uq[l], w_uk[l], kv_norm_g[l],
                   q_abs_norm_g[l], w_uv[l], w_iq[l], idx_k_norm_g[l], idx_k_norm_b[l], tb, conv_w[l],
                   conv_b[l], attn_out_norm_g[l], conv_out_norm_g[l], w_out[l], w_mlp1[l], b_mlp1[l],
                   w_mlp2[l], b_mlp2[l])
    return x
```

```python
import functools
import math

import numpy as np
import jax
import jax.numpy as jnp
from jax import lax
from jax.experimental import pallas as pl
from jax.experimental.pallas import tpu as pltpu

F32 = jnp.float32
BF16 = jnp.bfloat16

ATTN_HEADS = 8
HEAD_DIM = 128
Q_LORA = 512
KV_LORA = 256
IDX_HEADS = 16
IDX_DIM = 64
IDX_TOPK_MAX = 256
CONV_WIDTH = 1024
CONV_GROUPS = 8
CONV_GROUP_DIM = 128
CONV_K = 3
N_BUCKETS = 32
MAX_DISTANCE = 128
EPS = 1e-6
LOG2E = math.log2(math.e)

LANES = 128
KEY_BLOCK = 256
MASK_NEG = -1e30
M_INIT = -3e38
INT_MIN = -(2 ** 31)
VMEM_LIMIT = 56 * 1024 * 1024

C_CQ = 0
C_KV = C_CQ + Q_LORA
C_KI = C_KV + KV_LORA
C_WI = C_KI + LANES
C_GB = C_WI + LANES
C_GC = C_GB + CONV_WIDTH
C_HC = C_GC + CONV_WIDTH
IN_COLS_PAD = C_HC + CONV_WIDTH


def _dot(a, b):
    return jnp.dot(a, b, preferred_element_type=F32)


def _rms(x, g):
    return x * lax.rsqrt(jnp.mean(x * x, axis=-1, keepdims=True) + EPS) * g


def _params(sem):
    return pltpu.CompilerParams(dimension_semantics=sem, vmem_limit_bytes=VMEM_LIMIT)


def _const_spec(shape):
    n = len(shape)
    return pl.BlockSpec(shape, lambda *_: (0,) * n, pipeline_mode=pl.Buffered(1))


def _mod_kernel(c_ref, w_ref, b_ref, o_ref):
    c = c_ref[...]
    c_act = c / (1.0 + jnp.exp(-c))
    o_ref[...] = _dot(c_act, w_ref[...]) + b_ref[...]


def _mod_call(c8, w_ada, b_ada):
    d, n = w_ada.shape
    tn = 1024
    return pl.pallas_call(
        _mod_kernel,
        grid=(n // tn,),
        in_specs=[pl.BlockSpec((8, d), lambda j: (0, 0)),
                  pl.BlockSpec((d, tn), lambda j: (0, j)),
                  pl.BlockSpec((1, tn), lambda j: (0, j))],
        out_specs=pl.BlockSpec((8, tn), lambda j: (0, j)),
        out_shape=jax.ShapeDtypeStruct((8, n), F32),
        compiler_params=_params(("parallel",)),
        name="mod",
    )(c8, w_ada, b_ada)


def _in_kernel(x_ref, mod_ref, g1_ref, w_ref, cqg_ref, kvg_ref, kig_ref, kib_ref, cw_ref, cb_ref,
               cg_ref, cq_ref, ckv_ref, ki_ref, wi_ref, yc_ref, ubuf_ref, *, tm):
    x = x_ref[0]
    h = _rms(x, g1_ref[...]) * (1.0 + mod_ref[0, 1:2, :]) + mod_ref[0, 0:1, :]
    hb = h.astype(BF16)

    cq_ref[0] = _rms(_dot(hb, w_ref[:, C_CQ:C_CQ + Q_LORA]), cqg_ref[...]).astype(BF16)
    ckv_ref[0] = _rms(_dot(hb, w_ref[:, C_KV:C_KV + KV_LORA]), kvg_ref[...]).astype(BF16)

    ki = _dot(hb, w_ref[:, C_KI:C_KI + LANES])
    real = lax.broadcasted_iota(jnp.int32, (1, LANES), 1) < IDX_DIM
    mu = jnp.sum(jnp.where(real, ki, 0.0), axis=-1, keepdims=True) * (1.0 / IDX_DIM)
    xc = jnp.where(real, ki - mu, 0.0)
    var = jnp.sum(xc * xc, axis=-1, keepdims=True) * (1.0 / IDX_DIM)
    ki_ref[0] = (xc * lax.rsqrt(var + EPS) * kig_ref[...] + kib_ref[...]).astype(BF16)

    wi_ref[0] = _dot(hb, w_ref[:, C_WI:C_WI + LANES]) * (IDX_HEADS ** -0.5 * IDX_DIM ** -0.5)

    @pl.when(pl.program_id(1) == 0)
    def _():
        ubuf_ref[0:8, :] = jnp.zeros((8, CONV_WIDTH), F32)

    half = CONV_WIDTH // 2
    for c0 in (0, half):
        gc = _dot(hb, w_ref[:, C_GC + c0:C_GC + c0 + half])
        hc = _dot(hb, w_ref[:, C_HC + c0:C_HC + c0 + half])
        ubuf_ref[8:8 + tm, c0:c0 + half] = gc * hc
    for c0 in (0, half):
        gb = _dot(hb, w_ref[:, C_GB + c0:C_GB + c0 + half])
        y = cb_ref[:, c0:c0 + half]
        for j in range(CONV_K):
            lo = 8 - (CONV_K - 1) + j
            y = y + cw_ref[j:j + 1, c0:c0 + half] * ubuf_ref[lo:lo + tm, c0:c0 + half]
        y = gb * y
        for g in range(half // CONV_GROUP_DIM):
            l0 = g * CONV_GROUP_DIM
            yg = y[:, l0:l0 + CONV_GROUP_DIM]
            yc_ref[0, :, c0 + l0:c0 + l0 + CONV_GROUP_DIM] = _rms(
                yg, cg_ref[:, c0 + l0:c0 + l0 + CONV_GROUP_DIM]).astype(BF16)
    ubuf_ref[0:8, :] = ubuf_ref[tm:tm + 8, :]


def _in_call(x, mod3, norm1_g, w_in_r, cq_g, kv_g, ki_g, ki_b, conv_w, conv_b, conv_g, tm):
    bn, s, d = x.shape
    kern = functools.partial(_in_kernel, tm=tm)
    tok = lambda w: pl.BlockSpec((1, tm, w), lambda b, i: (b, i, 0))
    return pl.pallas_call(
        kern,
        grid=(bn, s // tm),
        in_specs=[tok(d),
                  pl.BlockSpec((1, 6, d), lambda b, i: (b, 0, 0)),
                  _const_spec((1, d)),
                  _const_spec((d, IN_COLS_PAD)),
                  _const_spec((1, Q_LORA)),
                  _const_spec((1, KV_LORA)),
                  _const_spec((1, LANES)),
                  _const_spec((1, LANES)),
                  _const_spec((CONV_K, CONV_WIDTH)),
                  _const_spec((1, CONV_WIDTH)),
                  _const_spec((1, CONV_WIDTH))],
        out_specs=[tok(Q_LORA), tok(KV_LORA), tok(LANES), tok(LANES), tok(CONV_WIDTH)],
        out_shape=[jax.ShapeDtypeStruct((bn, s, Q_LORA), BF16),
                   jax.ShapeDtypeStruct((bn, s, KV_LORA), BF16),
                   jax.ShapeDtypeStruct((bn, s, LANES), BF16),
                   jax.ShapeDtypeStruct((bn, s, LANES), F32),
                   jax.ShapeDtypeStruct((bn, s, CONV_WIDTH), BF16)],
        scratch_shapes=[pltpu.VMEM((tm + 8, CONV_WIDTH), F32)],
        compiler_params=_params(("parallel", "arbitrary")),
        name="in_proj",
    )(x, mod3, norm1_g, w_in_r, cq_g, kv_g, ki_g, ki_b, conv_w, conv_b, conv_g)


def _q_kernel(cq_ref, wuq_ref, wuk_ref, qg_ref, wiqt_ref, qa_ref, iqt_ref):
    cq = cq_ref[0]
    q = _dot(cq, wuq_ref[...]).astype(BF16)
    for h in range(ATTN_HEADS):
        qa = _dot(q[:, h * HEAD_DIM:(h + 1) * HEAD_DIM], wuk_ref[h])
        qa_ref[0, h] = (_rms(qa, qg_ref[...]) * (KV_LORA ** -0.5 * LOG2E)).astype(BF16)
    iqt = lax.dot_general(wiqt_ref[...], cq, (((1,), (1,)), ((), ())), preferred_element_type=F32)
    iqt_ref[0] = iqt.astype(BF16).reshape(IDX_HEADS, LANES, cq.shape[0])


def _q_call(cq, w_uq, w_uk, qa_g, w_iq_t, tq):
    bn, s, _ = cq.shape
    return pl.pallas_call(
        _q_kernel,
        grid=(bn, s // tq),
        in_specs=[pl.BlockSpec((1, tq, Q_LORA), lambda b, i: (b, i, 0)),
                  _const_spec((Q_LORA, ATTN_HEADS * HEAD_DIM)),
                  _const_spec((ATTN_HEADS, HEAD_DIM, KV_LORA)),
                  _const_spec((1, KV_LORA)),
                  _const_spec((IDX_HEADS * LANES, Q_LORA))],
        out_specs=[pl.BlockSpec((1, ATTN_HEADS, tq, KV_LORA), lambda b, i: (b, 0, i, 0)),
                   pl.BlockSpec((1, IDX_HEADS, LANES, tq), lambda b, i: (b, 0, 0, i))],
        out_shape=[jax.ShapeDtypeStruct((bn, ATTN_HEADS, s, KV_LORA), BF16),
                   jax.ShapeDtypeStruct((bn, IDX_HEADS, LANES, s), BF16)],
        compiler_params=_params(("parallel", "parallel")),
        name="q_proj",
    )(cq, w_uq, w_uk, qa_g, w_iq_t)


IDX_QB = 256
IDX_HALVES = IDX_QB // LANES
SUBLANES = 8
INT_MAX = 2 ** 31 - 1
SEARCH_MAX_STEPS = 2 + 32
SEARCH_STEPS_PER_TEST = 4


def _idx_kernel(iqt_ref, k_ref, wt_ref, bias_ref, key_ref, *, topk, nkb):
    i = pl.program_id(1)
    nch = i + 1
    nsu = (nch + 1) // 2
    nv = KEY_BLOCK // SUBLANES
    unit = 2 * KEY_BLOCK
    nvu = unit // SUBLANES
    halves = range(IDX_HALVES)
    lane = lax.broadcasted_iota(jnp.int32, (1, LANES), 1)

    def score_unit(u, kmax):
        k0 = pl.multiple_of(u * unit, unit)
        k = k_ref[0, pl.ds(k0, unit), :]
        kpos = k0 + lax.broadcasted_iota(jnp.int32, (unit, LANES), 0)
        new_max = []
        for hf in halves:
            q0 = hf * LANES
            acc = jnp.zeros((unit, LANES), F32)
            for hp in range(IDX_HEADS // 2):
                wt = jnp.concatenate([iqt_ref[0, 2 * hp, :, q0:q0 + LANES],
                                      iqt_ref[0, 2 * hp + 1, :, q0:q0 + LANES]], axis=1)
                lg = jnp.maximum(_dot(k, wt), 0.0)
                acc = acc + lg[:, :LANES] * wt_ref[0, 2 * hp:2 * hp + 1, q0:q0 + LANES]
                acc = acc + lg[:, LANES:] * wt_ref[0, 2 * hp + 1:2 * hp + 2, q0:q0 + LANES]
            bits = pltpu.bitcast(acc, jnp.int32)
            key = bits ^ ((bits >> 31) & 0x7FFFFFFF)
            key = jnp.where(kpos <= i * IDX_QB + q0 + lane, key, INT_MIN)
            key_ref[hf, pl.ds(k0, unit), :] = key
            new_max.append(jnp.maximum(kmax[hf], jnp.max(key.reshape(nvu, SUBLANES, LANES), axis=0)))
        return tuple(new_max)

    kmax = lax.fori_loop(0, nsu, score_unit,
                         tuple(jnp.full((SUBLANES, LANES), INT_MIN, jnp.int32) for _ in halves))

    k_f = float(topk)

    def counts(cands):
        def body(u, cnts):
            k0 = pl.multiple_of(u * unit, unit)
            out = []
            for hf in halves:
                v = key_ref[hf, pl.ds(k0, unit), :].reshape(nvu, SUBLANES, LANES)
                out.append(cnts[hf] + jnp.sum(jnp.where(v >= cands[hf][None], 1, 0), axis=0))
            return tuple(out)
        cnts = lax.fori_loop(0, nsu, body, tuple(jnp.zeros((SUBLANES, LANES), jnp.int32) for _ in halves))
        return [jnp.sum(c.astype(F32), axis=0, keepdims=True) for c in cnts]

    def finished(lo, hi, flo):
        return (flo <= k_f) | (hi - 1 <= lo)

    def pending(los, his, flos):
        return functools.reduce(jnp.maximum, [jnp.max(jnp.where(finished(los[hf], his[hf], flos[hf]), 0, 1))
                                              for hf in halves])

    los, his, flos, probes = [], [], [], []
    for hf in halves:
        kmx = jnp.broadcast_to(jnp.max(kmax[hf], axis=0, keepdims=True), (SUBLANES, LANES))
        los.append(jnp.full((SUBLANES, LANES), INT_MIN + 1, jnp.int32))
        his.append(kmx + jnp.where(kmx < INT_MAX, 1, 0))
        n_causal = i * IDX_QB + hf * LANES + lane + 1
        flos.append(jnp.broadcast_to(n_causal.astype(F32), (SUBLANES, LANES)))
        vmax = pltpu.bitcast(jnp.where(kmx < 0, kmx ^ 0x7FFFFFFF, kmx), F32)
        probes.append(pltpu.bitcast(jnp.maximum(vmax, 0.0) * 0.0625, jnp.int32))

    def search_cond(state):
        it, _, _, _, todo = state
        return (todo > 0) & (it < SEARCH_MAX_STEPS)

    def search_round(state):
        it, los, his, flos, _ = state
        for _ in range(SEARCH_STEPS_PER_TEST):
            it, los, his, flos = search_step(it, los, his, flos)
        return it, los, his, flos, pending(los, his, flos)

    def search_step(it, los, his, flos):
        cands = []
        for hf in halves:
            lo, hi = los[hf], his[hf]
            mid = (lo & hi) + ((lo ^ hi) >> 1)
            t = jnp.where(it == 0, 0, jnp.where(it == 1, probes[hf], mid))
            t = jnp.minimum(jnp.maximum(t, lo + 1), hi - 1)
            cands.append(jnp.where(finished(lo, hi, flos[hf]), lo, t))
        cs = counts(cands)
        nlo, nhi, nflo = [], [], []
        for hf in halves:
            live = jnp.logical_not(finished(los[hf], his[hf], flos[hf]))
            ok = cs[hf] >= k_f
            nlo.append(jnp.where(live & ok, cands[hf], los[hf]))
            nflo.append(jnp.where(live & ok, cs[hf], flos[hf]))
            nhi.append(jnp.where(live & jnp.logical_not(ok), cands[hf], his[hf]))
        return it + 1, tuple(nlo), tuple(nhi), tuple(nflo)

    state = (jnp.int32(0), tuple(los), tuple(his), tuple(flos), pending(los, his, flos))
    thr = lax.while_loop(search_cond, search_round, state)[1]

    def write_chunk(c, carry):
        k0 = pl.multiple_of(c * KEY_BLOCK, KEY_BLOCK)
        for hf in halves:
            k = key_ref[hf, pl.ds(k0, KEY_BLOCK), :].reshape(nv, SUBLANES, LANES)
            m = jnp.where(k >= thr[hf][None], 0.0, MASK_NEG).reshape(KEY_BLOCK, LANES)
            bias_ref[0, c, hf * LANES:(hf + 1) * LANES, :] = m.T.astype(BF16)
        return carry

    lax.fori_loop(0, nch, write_chunk, 0)

    def fill_chunk(c, carry):
        bias_ref[0, c] = jnp.full((IDX_QB, KEY_BLOCK), MASK_NEG, BF16)
        return carry

    lax.fori_loop(nch, nkb, fill_chunk, 0)


def _idx_call(iq_t, kidx, widx_t, topk):
    bn, s, _ = kidx.shape
    nkb = s // KEY_BLOCK
    kern = functools.partial(_idx_kernel, topk=topk, nkb=nkb)
    return pl.pallas_call(
        kern,
        grid=(bn, s // IDX_QB),
        in_specs=[pl.BlockSpec((1, IDX_HEADS, LANES, IDX_QB), lambda b, i: (b, 0, 0, i)),
                  pl.BlockSpec((1, s, LANES), lambda b, i: (b, 0, 0)),
                  pl.BlockSpec((1, IDX_HEADS, IDX_QB), lambda b, i: (b, 0, i))],
        out_specs=pl.BlockSpec((1, nkb, IDX_QB, KEY_BLOCK), lambda b, i: (b, 0, i, 0)),
        out_shape=jax.ShapeDtypeStruct((bn, nkb, s, KEY_BLOCK), BF16),
        scratch_shapes=[pltpu.VMEM((IDX_HALVES, s, LANES), jnp.int32)],
        compiler_params=_params(("parallel", "arbitrary")),
        name="indexer",
    )(iq_t, kidx, widx_t)


ATT_QB = 256


def _attn_kernel(q_ref, kt_ref, v_ref, bias_ref, tb_ref, wuv_ref, og_ref, out_ref, m_ref, l_ref, acc_ref):
    i = pl.program_id(1)
    rows = ATTN_HEADS * ATT_QB
    nl = KV_LORA // LANES

    m_ref[...] = jnp.full((rows, LANES), M_INIT, F32)
    l_ref[...] = jnp.zeros((rows, LANES), F32)
    acc_ref[...] = jnp.zeros((rows, KV_LORA), F32)

    def block(kb, nb, tb_col):
        width = nb * KEY_BLOCK
        kt = jnp.concatenate([kt_ref[0, kb + j] for j in range(nb)], axis=1)
        v = v_ref[0, pl.ds(pl.multiple_of(kb * KEY_BLOCK, KEY_BLOCK), width), :]
        bias = jnp.concatenate([bias_ref[0, kb + j] for j in range(nb)], axis=1).astype(F32)
        for h in range(ATTN_HEADS):
            r0 = h * ATT_QB
            s = _dot(q_ref[0, h], kt) + bias
            if tb_col is not None:
                s = s + tb_ref[h, :, tb_col:tb_col + width]
            m_old = m_ref[r0:r0 + ATT_QB, :]
            m_new = jnp.maximum(m_old, jnp.max(s, axis=-1, keepdims=True))
            alpha = jnp.exp2(m_old - m_new)
            ps = [jnp.exp2(s[:, c * LANES:(c + 1) * LANES] - m_new) for c in range(width // LANES)]
            l_ref[r0:r0 + ATT_QB, :] = alpha * l_ref[r0:r0 + ATT_QB, :] + functools.reduce(jnp.add, ps)
            m_ref[r0:r0 + ATT_QB, :] = m_new
            pv = _dot(jnp.concatenate(ps, axis=1).astype(BF16), v)
            acc_ref[r0:r0 + ATT_QB, :] = acc_ref[r0:r0 + ATT_QB, :] * jnp.concatenate([alpha] * nl, axis=1) + pv

    nfar = jnp.maximum(i - 1, 0)

    def far(j, carry):
        block(2 * j, 2, None)
        return carry

    lax.fori_loop(0, nfar // 2, far, 0)

    @pl.when(nfar % 2 == 1)
    def _():
        block(nfar - 1, 1, None)

    @pl.when(i > 0)
    def _():
        block(i - 1, 2, 0)

    @pl.when(i == 0)
    def _():
        block(0, 1, KEY_BLOCK)

    for h in range(ATTN_HEADS):
        r0 = h * ATT_QB
        inv = 1.0 / jnp.sum(l_ref[r0:r0 + ATT_QB, :], axis=-1, keepdims=True)
        o = acc_ref[r0:r0 + ATT_QB, :] * inv
        y = _dot(o.astype(BF16), wuv_ref[h])
        out_ref[0, :, h * HEAD_DIM:(h + 1) * HEAD_DIM] = _rms(y, og_ref[h:h + 1, :]).astype(BF16)


def _attn_call(q_abs, ckv_t, ckv, bias, tb, w_uv, o_g):
    bn, _, s, _ = q_abs.shape
    nkb = s // KEY_BLOCK
    rows = ATTN_HEADS * ATT_QB
    return pl.pallas_call(
        _attn_kernel,
        grid=(bn, s // ATT_QB),
        in_specs=[pl.BlockSpec((1, ATTN_HEADS, ATT_QB, KV_LORA), lambda b, i: (b, 0, i, 0)),
                  pl.BlockSpec((1, nkb, KV_LORA, KEY_BLOCK), lambda b, i: (b, 0, 0, 0)),
                  pl.BlockSpec((1, s, KV_LORA), lambda b, i: (b, 0, 0)),
                  pl.BlockSpec((1, nkb, ATT_QB, KEY_BLOCK), lambda b, i: (b, 0, i, 0)),
                  _const_spec((ATTN_HEADS, ATT_QB, 2 * KEY_BLOCK)),
                  _const_spec((ATTN_HEADS, KV_LORA, HEAD_DIM)),
                  _const_spec((ATTN_HEADS, HEAD_DIM))],
        out_specs=pl.BlockSpec((1, ATT_QB, ATTN_HEADS * HEAD_DIM), lambda b, i: (b, i, 0)),
        out_shape=jax.ShapeDtypeStruct((bn, s, ATTN_HEADS * HEAD_DIM), BF16),
        scratch_shapes=[pltpu.VMEM((rows, LANES), F32),
                        pltpu.VMEM((rows, LANES), F32),
                        pltpu.VMEM((rows, KV_LORA), F32)],
        compiler_params=_params(("parallel", "arbitrary")),
        name="attn",
    )(q_abs, ckv_t, ckv, bias, tb, w_uv, o_g)


def _out_kernel(x_ref, ya_ref, yc_ref, w_ref, mod_ref, g2_ref, x1_ref, h2_ref):
    half = w_ref.shape[0] // 2
    y = _dot(ya_ref[0], w_ref[0:half, :]) + _dot(yc_ref[0], w_ref[half:, :])
    x1 = x_ref[0] + mod_ref[0, 2:3, :] * y
    x1_ref[0] = x1
    h2_ref[0] = (_rms(x1, g2_ref[...]) * (1.0 + mod_ref[0, 4:5, :]) + mod_ref[0, 3:4, :]).astype(BF16)


def _out_call(x, y_attn, y_conv, w_out, mod3, norm2_g, tm):
    bn, s, d = x.shape
    half = y_attn.shape[-1]
    tok = lambda w: pl.BlockSpec((1, tm, w), lambda b, i: (b, i, 0))
    return pl.pallas_call(
        _out_kernel,
        grid=(bn, s // tm),
        in_specs=[tok(d), tok(half), tok(half),
                  _const_spec((2 * half, d)),
                  pl.BlockSpec((1, 6, d), lambda b, i: (b, 0, 0)),
                  _const_spec((1, d))],
        out_specs=[tok(d), tok(d)],
        out_shape=[jax.ShapeDtypeStruct((bn, s, d), F32),
                   jax.ShapeDtypeStruct((bn, s, d), BF16)],
        compiler_params=_params(("parallel", "parallel")),
        name="out_proj",
    )(x, y_attn, y_conv, w_out, mod3, norm2_g)


MLP_SUBCHUNKS = 2


def _mlp_kernel(h2_ref, x1_ref, w1_ref, b1_ref, w2_ref, b2_ref, mod_ref, o_ref, acc_ref):
    f = pl.program_id(2)

    @pl.when(f == 0)
    def _():
        acc_ref[...] = jnp.zeros(acc_ref.shape, F32)

    sub = w1_ref.shape[1] // MLP_SUBCHUNKS
    contrib = None
    for j in range(MLP_SUBCHUNKS):
        a = jnp.maximum(_dot(h2_ref[0], w1_ref[:, j * sub:(j + 1) * sub]) + b1_ref[:, j * sub:(j + 1) * sub], 0.0)
        part = _dot((a * a).astype(BF16), w2_ref[j * sub:(j + 1) * sub, :])
        contrib = part if contrib is None else contrib + part
    acc_ref[...] += contrib

    @pl.when(f == pl.num_programs(2) - 1)
    def _():
        o_ref[0] = x1_ref[0] + mod_ref[0, 5:6, :] * (acc_ref[...] + b2_ref[...])


def _mlp_call(h2, x1, w1, b1, w2, b2, mod3, tm, tf):
    bn, s, d = x1.shape
    dff = w1.shape[1]
    return pl.pallas_call(
        _mlp_kernel,
        grid=(bn, s // tm, dff // tf),
        in_specs=[pl.BlockSpec((1, tm, d), lambda b, i, f: (b, i, 0)),
                  pl.BlockSpec((1, tm, d), lambda b, i, f: (b, i, 0)),
                  pl.BlockSpec((d, tf), lambda b, i, f: (0, f)),
                  pl.BlockSpec((1, tf), lambda b, i, f: (0, f)),
                  pl.BlockSpec((tf, d), lambda b, i, f: (f, 0)),
                  pl.BlockSpec((1, d), lambda b, i, f: (0, 0)),
                  pl.BlockSpec((1, 6, d), lambda b, i, f: (b, 0, 0))],
        out_specs=pl.BlockSpec((1, tm, d), lambda b, i, f: (b, i, 0)),
        out_shape=jax.ShapeDtypeStruct((bn, s, d), F32),
        scratch_shapes=[pltpu.VMEM((tm, d), F32)],
        compiler_params=_params(("parallel", "parallel", "arbitrary")),
        name="mlp",
    )(h2, x1, w1, b1, w2, b2, mod3)


def _t5_bucket_table():
    n = np.arange(2 * KEY_BLOCK, dtype=np.int64)
    max_exact = N_BUCKETS // 2
    nf = np.maximum(n, 1).astype(np.float32)
    ratio = np.log(nf / np.float32(max_exact)) / np.float32(math.log(MAX_DISTANCE / max_exact))
    large = max_exact + (ratio * np.float32(N_BUCKETS - max_exact)).astype(np.int32)
    large = np.minimum(large, N_BUCKETS - 1)
    bucket = np.where(n < max_exact, n, large)
    assert (bucket[MAX_DISTANCE:] == N_BUCKETS - 1).all()
    return bucket


def _near_bias_table(rel_bias):
    r = np.arange(ATT_QB)[:, None]
    c = np.arange(2 * KEY_BLOCK)[None, :]
    dist = np.clip(KEY_BLOCK + r - c, 0, 2 * KEY_BLOCK - 1)
    idx = jnp.asarray(_t5_bucket_table()[dist])
    rel_t = rel_bias.astype(F32).T * LOG2E
    tb = jnp.zeros((rel_t.shape[0],) + idx.shape, F32)
    for b in range(N_BUCKETS - 1):
        tb = jnp.where(idx[None] == b, (rel_t[:, b] - rel_t[:, N_BUCKETS - 1])[:, None, None], tb)
    return tb


def _layer(x, mod, norm1_g, norm2_g, w_in, cq_g, w_uq, w_uk, kv_g, qa_g, w_uv, w_iq, ki_g, ki_b, tb,
           conv_w, conv_b, ao_g, co_g, w_out, w_mlp1, b_mlp1, w_mlp2, b_mlp2):
    bn, s, d = x.shape
    topk = min(IDX_TOPK_MAX, s // 4)
    nkb = s // KEY_BLOCK
    mod3 = mod.reshape(bn, 6, d)

    o1 = Q_LORA + KV_LORA
    o2 = o1 + IDX_DIM
    o3 = o2 + IDX_HEADS
    zpad = lambda n: jnp.zeros((d, n), F32)
    w_in_r = jnp.concatenate(
        [w_in[:, :o1], w_in[:, o1:o2], zpad(LANES - IDX_DIM), w_in[:, o2:o3], zpad(LANES - IDX_HEADS),
         w_in[:, o3:]], axis=1).astype(BF16)
    pad_lanes = lambda v: jnp.pad(v, (0, LANES - v.shape[0])).reshape(1, LANES)
    w_iq_t = jnp.pad(w_iq.reshape(Q_LORA, IDX_HEADS, IDX_DIM).transpose(1, 2, 0),
                     ((0, 0), (0, LANES - IDX_DIM), (0, 0))).astype(BF16).reshape(
                         IDX_HEADS * LANES, Q_LORA)

    tm = min(512, s)
    cq, ckv, kidx, widx, y_conv = _in_call(
        x, mod3, norm1_g.reshape(1, d), w_in_r, cq_g.reshape(1, -1), kv_g.reshape(1, -1),
        pad_lanes(ki_g), pad_lanes(ki_b), conv_w, conv_b.reshape(1, -1), co_g.reshape(1, -1), tm)

    q_abs, iq_t = _q_call(cq, w_uq.astype(BF16), w_uk.astype(BF16), qa_g.reshape(1, -1), w_iq_t, tm)

    widx_t = widx[:, :, :IDX_HEADS].swapaxes(1, 2)
    bias = _idx_call(iq_t, kidx, widx_t, topk)

    ckv_t = ckv.reshape(bn, nkb, KEY_BLOCK, KV_LORA).swapaxes(2, 3)
    y_attn = _attn_call(q_abs, ckv_t, ckv, bias, tb, w_uv.astype(BF16), ao_g)

    x1, h2 = _out_call(x, y_attn, y_conv, w_out.astype(BF16), mod3, norm2_g.reshape(1, d), tm)
    return _mlp_call(h2, x1, w_mlp1.astype(BF16), b_mlp1.reshape(1, -1), w_mlp2.astype(BF16),
                     b_mlp2.reshape(1, -1), mod3, tm, 1024)


def kernel(x, c, w_ada, b_ada, norm1_g, norm2_g, w_in, cq_norm_g, w_uq, w_uk, kv_norm_g, q_abs_norm_g,
           w_uv, w_iq, idx_k_norm_g, idx_k_norm_b, rel_bias, conv_w, conv_b, attn_out_norm_g,
           conv_out_norm_g, w_out, w_mlp1, b_mlp1, w_mlp2, b_mlp2):
    bn = x.shape[0]
    depth = w_ada.shape[0]
    assert x.shape[1] % (2 * KEY_BLOCK) == 0 and bn <= 8
    tb = _near_bias_table(rel_bias)
    c8 = jnp.pad(c, ((0, 8 - bn), (0, 0)))
    for l in range(depth):
        mod = _mod_call(c8, w_ada[l], b_ada[l].reshape(1, -1))[:bn]
        x = _layer(x, mod, norm1_g[l], norm2_g[l], w_in[l], cq_norm_g[l], w_uq[l], w_uk[l], kv_norm_g[l],
                   q_abs_norm_g[l], w_uv[l], w_iq[l], idx_k_norm_g[l], idx_k_norm_b[l], tb, conv_w[l],
                   conv_b[l], attn_out_norm_g[l], conv_out_norm_g[l], w_out[l], w_mlp1[l], b_mlp1[l],
                   w_mlp2[l], b_mlp2[l])
    return x
```

```python
import functools
import math

import numpy as np
import jax
import jax.numpy as jnp
from jax import lax
from jax.experimental import pallas as pl
from jax.experimental.pallas import tpu as pltpu

F32 = jnp.float32
BF16 = jnp.bfloat16

ATTN_HEADS = 8
HEAD_DIM = 128
Q_LORA = 512
KV_LORA = 256
IDX_HEADS = 16
IDX_DIM = 64
IDX_TOPK_MAX = 256
CONV_WIDTH = 1024
CONV_GROUPS = 8
CONV_GROUP_DIM = 128
CONV_K = 3
N_BUCKETS = 32
MAX_DISTANCE = 128
EPS = 1e-6
LOG2E = math.log2(math.e)

LANES = 128
KEY_BLOCK = 256
MASK_NEG = -1e30
M_INIT = -3e38
INT_MIN = -(2 ** 31)
VMEM_LIMIT = 56 * 1024 * 1024

C_CQ = 0
C_KV = C_CQ + Q_LORA
C_KI = C_KV + KV_LORA
C_GB = C_KI + LANES
C_GC = C_GB + CONV_WIDTH
C_HC = C_GC + CONV_WIDTH
IN_COLS_PAD = C_HC + CONV_WIDTH


def _dot(a, b):
    return jnp.dot(a, b, preferred_element_type=F32)


def _rms(x, g):
    return x * lax.rsqrt(jnp.mean(x * x, axis=-1, keepdims=True) + EPS) * g


def _params(sem, vmem_limit=VMEM_LIMIT):
    return pltpu.CompilerParams(dimension_semantics=sem, vmem_limit_bytes=vmem_limit)


def _const_spec(shape):
    n = len(shape)
    return pl.BlockSpec(shape, lambda *_: (0,) * n, pipeline_mode=pl.Buffered(1))


def _mod_kernel(c_ref, w_ref, b_ref, o_ref):
    c = c_ref[...]
    c_act = c / (1.0 + jnp.exp(-c))
    o_ref[...] = _dot(c_act, w_ref[...]) + b_ref[...]


def _mod_call(c8, w_ada, b_ada):
    d, n = w_ada.shape
    tn = 1024
    return pl.pallas_call(
        _mod_kernel,
        grid=(n // tn,),
        in_specs=[pl.BlockSpec((8, d), lambda j: (0, 0)),
                  pl.BlockSpec((d, tn), lambda j: (0, j)),
                  pl.BlockSpec((1, tn), lambda j: (0, j))],
        out_specs=pl.BlockSpec((8, tn), lambda j: (0, j)),
        out_shape=jax.ShapeDtypeStruct((8, n), F32),
        compiler_params=_params(("parallel",)),
        name="mod",
    )(c8, w_ada, b_ada)


def _in_kernel(x_ref, mod_ref, g1_ref, w_ref, cqg_ref, kvg_ref, kig_ref, kib_ref, cw_ref, cb_ref,
               cg_ref, cq_ref, ckv_ref, ki_ref, wi_ref, yc_ref, ubuf_ref, *, tm):
    x = x_ref[0]
    h = _rms(x, g1_ref[...]) * (1.0 + mod_ref[0, 1:2, :]) + mod_ref[0, 0:1, :]
    hb = h.astype(BF16)

    cq_ref[0] = _rms(_dot(hb, w_ref[:, C_CQ:C_CQ + Q_LORA]), cqg_ref[...]).astype(BF16)
    ckv_ref[0] = _rms(_dot(hb, w_ref[:, C_KV:C_KV + KV_LORA]), kvg_ref[...]).astype(BF16)

    ki = _dot(hb, w_ref[:, C_KI:C_KI + LANES])
    real = lax.broadcasted_iota(jnp.int32, (1, LANES), 1) < IDX_DIM
    mu = jnp.sum(jnp.where(real, ki, 0.0), axis=-1, keepdims=True) * (1.0 / IDX_DIM)
    xc = jnp.where(real, ki - mu, 0.0)
    var = jnp.sum(xc * xc, axis=-1, keepdims=True) * (1.0 / IDX_DIM)
    ki_ref[0] = (xc * lax.rsqrt(var + EPS) * kig_ref[...] + kib_ref[...]).astype(BF16)

    wi_ref[0] = ki * (IDX_HEADS ** -0.5 * IDX_DIM ** -0.5)

    @pl.when(pl.program_id(1) == 0)
    def _():
        ubuf_ref[0:8, :] = jnp.zeros((8, CONV_WIDTH), F32)

    half = CONV_WIDTH // 2
    for c0 in (0, half):
        gc = _dot(hb, w_ref[:, C_GC + c0:C_GC + c0 + half])
        hc = _dot(hb, w_ref[:, C_HC + c0:C_HC + c0 + half])
        ubuf_ref[8:8 + tm, c0:c0 + half] = gc * hc
    for c0 in (0, half):
        gb = _dot(hb, w_ref[:, C_GB + c0:C_GB + c0 + half])
        y = cb_ref[:, c0:c0 + half]
        for j in range(CONV_K):
            lo = 8 - (CONV_K - 1) + j
            y = y + cw_ref[j:j + 1, c0:c0 + half] * ubuf_ref[lo:lo + tm, c0:c0 + half]
        y = gb * y
        for g in range(half // CONV_GROUP_DIM):
            l0 = g * CONV_GROUP_DIM
            yg = y[:, l0:l0 + CONV_GROUP_DIM]
            yc_ref[0, :, c0 + l0:c0 + l0 + CONV_GROUP_DIM] = _rms(
                yg, cg_ref[:, c0 + l0:c0 + l0 + CONV_GROUP_DIM]).astype(BF16)
    ubuf_ref[0:8, :] = ubuf_ref[tm:tm + 8, :]


def _in_call(x, mod3, norm1_g, w_in_r, cq_g, kv_g, ki_g, ki_b, conv_w, conv_b, conv_g, tm):
    bn, s, d = x.shape
    kern = functools.partial(_in_kernel, tm=tm)
    tok = lambda w: pl.BlockSpec((1, tm, w), lambda b, i: (b, i, 0))
    return pl.pallas_call(
        kern,
        grid=(bn, s // tm),
        in_specs=[tok(d),
                  pl.BlockSpec((1, 6, d), lambda b, i: (b, 0, 0)),
                  _const_spec((1, d)),
                  _const_spec((d, IN_COLS_PAD)),
                  _const_spec((1, Q_LORA)),
                  _const_spec((1, KV_LORA)),
                  _const_spec((1, LANES)),
                  _const_spec((1, LANES)),
                  _const_spec((CONV_K, CONV_WIDTH)),
                  _const_spec((1, CONV_WIDTH)),
                  _const_spec((1, CONV_WIDTH))],
        out_specs=[tok(Q_LORA), tok(KV_LORA), tok(LANES), tok(LANES), tok(CONV_WIDTH)],
        out_shape=[jax.ShapeDtypeStruct((bn, s, Q_LORA), BF16),
                   jax.ShapeDtypeStruct((bn, s, KV_LORA), BF16),
                   jax.ShapeDtypeStruct((bn, s, LANES), BF16),
                   jax.ShapeDtypeStruct((bn, s, LANES), F32),
                   jax.ShapeDtypeStruct((bn, s, CONV_WIDTH), BF16)],
        scratch_shapes=[pltpu.VMEM((tm + 8, CONV_WIDTH), F32)],
        compiler_params=_params(("parallel", "arbitrary")),
        name="in_proj",
    )(x, mod3, norm1_g, w_in_r, cq_g, kv_g, ki_g, ki_b, conv_w, conv_b, conv_g)


def _q_kernel(cq_ref, wuq_ref, wuk_ref, qg_ref, wiqt_ref, qa_ref, iqt_ref):
    cq = cq_ref[0]
    q = _dot(cq, wuq_ref[...]).astype(BF16)
    for h in range(ATTN_HEADS):
        qa = _dot(q[:, h * HEAD_DIM:(h + 1) * HEAD_DIM], wuk_ref[h])
        qa_ref[0, h] = (_rms(qa, qg_ref[...]) * (KV_LORA ** -0.5 * LOG2E)).astype(BF16)
    iqt = lax.dot_general(wiqt_ref[...], cq, (((1,), (1,)), ((), ())), preferred_element_type=F32)
    iqt_ref[0] = iqt.astype(BF16).reshape(IDX_HEADS, LANES, cq.shape[0])


def _q_call(cq, w_uq, w_uk, qa_g, w_iq_t, tq):
    bn, s, _ = cq.shape
    return pl.pallas_call(
        _q_kernel,
        grid=(bn, s // tq),
        in_specs=[pl.BlockSpec((1, tq, Q_LORA), lambda b, i: (b, i, 0)),
                  _const_spec((Q_LORA, ATTN_HEADS * HEAD_DIM)),
                  _const_spec((ATTN_HEADS, HEAD_DIM, KV_LORA)),
                  _const_spec((1, KV_LORA)),
                  _const_spec((IDX_HEADS * LANES, Q_LORA))],
        out_specs=[pl.BlockSpec((1, ATTN_HEADS, tq, KV_LORA), lambda b, i: (b, 0, i, 0)),
                   pl.BlockSpec((1, IDX_HEADS, LANES, tq), lambda b, i: (b, 0, 0, i))],
        out_shape=[jax.ShapeDtypeStruct((bn, ATTN_HEADS, s, KV_LORA), BF16),
                   jax.ShapeDtypeStruct((bn, IDX_HEADS, LANES, s), BF16)],
        compiler_params=_params(("parallel", "parallel")),
        name="q_proj",
    )(cq, w_uq, w_uk, qa_g, w_iq_t)


IDX_QB = 256
IDX_HALVES = IDX_QB // LANES
SUBLANES = 8
INT_MAX = 2 ** 31 - 1
SEARCH_MAX_STEPS = 2 + 32
SEARCH_STEPS_PER_TEST = 4


def _idx_kernel(iqt_ref, k_ref, wt_ref, bias_ref, key_ref, *, topk, nkb):
    i = pl.program_id(1)
    nch = i + 1
    nsu = (nch + 1) // 2
    nv = KEY_BLOCK // SUBLANES
    unit = 2 * KEY_BLOCK
    nvu = unit // SUBLANES
    halves = range(IDX_HALVES)
    lane = lax.broadcasted_iota(jnp.int32, (1, LANES), 1)

    def score_unit(u, kmax):
        k0 = pl.multiple_of(u * unit, unit)
        k = k_ref[0, pl.ds(k0, unit), :]
        kpos = k0 + lax.broadcasted_iota(jnp.int32, (unit, LANES), 0)
        new_max = []
        for hf in halves:
            q0 = hf * LANES
            acc = jnp.zeros((unit, LANES), F32)
            for hp in range(IDX_HEADS // 2):
                wt = jnp.concatenate([iqt_ref[0, 2 * hp, :, q0:q0 + LANES],
                                      iqt_ref[0, 2 * hp + 1, :, q0:q0 + LANES]], axis=1)
                lg = jnp.maximum(_dot(k, wt), 0.0)
                acc = acc + lg[:, :LANES] * wt_ref[0, 2 * hp:2 * hp + 1, q0:q0 + LANES]
                acc = acc + lg[:, LANES:] * wt_ref[0, 2 * hp + 1:2 * hp + 2, q0:q0 + LANES]
            bits = pltpu.bitcast(acc, jnp.int32)
            key = bits ^ ((bits >> 31) & 0x7FFFFFFF)
            key = jnp.where(kpos <= i * IDX_QB + q0 + lane, key, INT_MIN)
            key_ref[hf, pl.ds(k0, unit), :] = key
            new_max.append(jnp.maximum(kmax[hf], jnp.max(key.reshape(nvu, SUBLANES, LANES), axis=0)))
        return tuple(new_max)

    kmax = lax.fori_loop(0, nsu, score_unit,
                         tuple(jnp.full((SUBLANES, LANES), INT_MIN, jnp.int32) for _ in halves))

    k_f = float(topk)

    def counts(cands):
        def body(u, cnts):
            k0 = pl.multiple_of(u * unit, unit)
            out = []
            for hf in halves:
                v = key_ref[hf, pl.ds(k0, unit), :].reshape(nvu, SUBLANES, LANES)
                out.append(cnts[hf] + jnp.sum(jnp.where(v >= cands[hf][None], 1, 0), axis=0))
            return tuple(out)
        cnts = lax.fori_loop(0, nsu, body, tuple(jnp.zeros((SUBLANES, LANES), jnp.int32) for _ in halves))
        return [jnp.sum(c.astype(F32), axis=0, keepdims=True) for c in cnts]

    def finished(lo, hi, flo):
        return (flo <= k_f) | (hi - 1 <= lo)

    def pending(los, his, flos):
        return functools.reduce(jnp.maximum, [jnp.max(jnp.where(finished(los[hf], his[hf], flos[hf]), 0, 1))
                                              for hf in halves])

    los, his, flos, probes = [], [], [], []
    for hf in halves:
        kmx = jnp.broadcast_to(jnp.max(kmax[hf], axis=0, keepdims=True), (SUBLANES, LANES))
        los.append(jnp.full((SUBLANES, LANES), INT_MIN + 1, jnp.int32))
        his.append(kmx + jnp.where(kmx < INT_MAX, 1, 0))
        n_causal = i * IDX_QB + hf * LANES + lane + 1
        flos.append(jnp.broadcast_to(n_causal.astype(F32), (SUBLANES, LANES)))
        vmax = pltpu.bitcast(jnp.where(kmx < 0, kmx ^ 0x7FFFFFFF, kmx), F32)
        probes.append(pltpu.bitcast(jnp.maximum(vmax, 0.0) * 0.0625, jnp.int32))

    def search_cond(state):
        it, _, _, _, todo = state
        return (todo > 0) & (it < SEARCH_MAX_STEPS)

    def search_round(state):
        it, los, his, flos, _ = state
        for _ in range(SEARCH_STEPS_PER_TEST):
            it, los, his, flos = search_step(it, los, his, flos)
        return it, los, his, flos, pending(los, his, flos)

    def search_step(it, los, his, flos):
        cands = []
        for hf in halves:
            lo, hi = los[hf], his[hf]
            mid = (lo & hi) + ((lo ^ hi) >> 1)
            t = jnp.where(it == 0, 0, jnp.where(it == 1, probes[hf], mid))
            t = jnp.minimum(jnp.maximum(t, lo + 1), hi - 1)
            cands.append(jnp.where(finished(lo, hi, flos[hf]), lo, t))
        cs = counts(cands)
        nlo, nhi, nflo = [], [], []
        for hf in halves:
            live = jnp.logical_not(finished(los[hf], his[hf], flos[hf]))
            ok = cs[hf] >= k_f
            nlo.append(jnp.where(live & ok, cands[hf], los[hf]))
            nflo.append(jnp.where(live & ok, cs[hf], flos[hf]))
            nhi.append(jnp.where(live & jnp.logical_not(ok), cands[hf], his[hf]))
        return it + 1, tuple(nlo), tuple(nhi), tuple(nflo)

    state = (jnp.int32(0), tuple(los), tuple(his), tuple(flos), pending(los, his, flos))
    thr = lax.while_loop(search_cond, search_round, state)[1]

    def write_chunk(c, carry):
        k0 = pl.multiple_of(c * KEY_BLOCK, KEY_BLOCK)
        for hf in halves:
            k = key_ref[hf, pl.ds(k0, KEY_BLOCK), :].reshape(nv, SUBLANES, LANES)
            m = jnp.where(k >= thr[hf][None], 0.0, MASK_NEG).reshape(KEY_BLOCK, LANES)
            bias_ref[0, c, hf * LANES:(hf + 1) * LANES, :] = m.T.astype(BF16)
        return carry

    lax.fori_loop(0, nch, write_chunk, 0)

    def fill_chunk(c, carry):
        bias_ref[0, c] = jnp.full((IDX_QB, KEY_BLOCK), MASK_NEG, BF16)
        return carry

    lax.fori_loop(nch, nkb, fill_chunk, 0)


def _idx_call(iq_t, kidx, widx_t, topk):
    bn, s, _ = kidx.shape
    nkb = s // KEY_BLOCK
    kern = functools.partial(_idx_kernel, topk=topk, nkb=nkb)
    return pl.pallas_call(
        kern,
        grid=(bn, s // IDX_QB),
        in_specs=[pl.BlockSpec((1, IDX_HEADS, LANES, IDX_QB), lambda b, i: (b, 0, 0, i)),
                  pl.BlockSpec((1, s, LANES), lambda b, i: (b, 0, 0)),
                  pl.BlockSpec((1, IDX_HEADS, IDX_QB), lambda b, i: (b, 0, i))],
        out_specs=pl.BlockSpec((1, nkb, IDX_QB, KEY_BLOCK), lambda b, i: (b, 0, i, 0)),
        out_shape=jax.ShapeDtypeStruct((bn, nkb, s, KEY_BLOCK), BF16),
        scratch_shapes=[pltpu.VMEM((IDX_HALVES, s, LANES), jnp.int32)],
        compiler_params=_params(("parallel", "arbitrary")),
        name="indexer",
    )(iq_t, kidx, widx_t)


ATT_QB = 256


def _attn_kernel(q_ref, kt_ref, v_ref, bias_ref, tb_ref, wuv_ref, og_ref, out_ref, m_ref, l_ref, acc_ref):
    i = pl.program_id(1)
    rows = ATTN_HEADS * ATT_QB
    nl = KV_LORA // LANES

    m_ref[...] = jnp.full((rows, LANES), M_INIT, F32)
    l_ref[...] = jnp.zeros((rows, LANES), F32)
    acc_ref[...] = jnp.zeros((rows, KV_LORA), F32)

    def block(kb, nb, tb_col):
        width = nb * KEY_BLOCK
        kt = jnp.concatenate([kt_ref[0, kb + j] for j in range(nb)], axis=1)
        v = v_ref[0, pl.ds(pl.multiple_of(kb * KEY_BLOCK, KEY_BLOCK), width), :]
        bias = jnp.concatenate([bias_ref[0, kb + j] for j in range(nb)], axis=1).astype(F32)
        for h in range(ATTN_HEADS):
            r0 = h * ATT_QB
            s = _dot(q_ref[0, h], kt) + bias
            if tb_col is not None:
                s = s + tb_ref[h, :, tb_col:tb_col + width]
            m_old = m_ref[r0:r0 + ATT_QB, :]
            m_new = jnp.maximum(m_old, jnp.max(s, axis=-1, keepdims=True))
            alpha = jnp.exp2(m_old - m_new)
            ps = [jnp.exp2(s[:, c * LANES:(c + 1) * LANES] - m_new) for c in range(width // LANES)]
            l_ref[r0:r0 + ATT_QB, :] = alpha * l_ref[r0:r0 + ATT_QB, :] + functools.reduce(jnp.add, ps)
            m_ref[r0:r0 + ATT_QB, :] = m_new
            pv = _dot(jnp.concatenate(ps, axis=1).astype(BF16), v)
            acc_ref[r0:r0 + ATT_QB, :] = acc_ref[r0:r0 + ATT_QB, :] * jnp.concatenate([alpha] * nl, axis=1) + pv

    nfar = jnp.maximum(i - 1, 0)

    def far(j, carry):
        block(2 * j, 2, None)
        return carry

    lax.fori_loop(0, nfar // 2, far, 0)

    @pl.when(nfar % 2 == 1)
    def _():
        block(nfar - 1, 1, None)

    @pl.when(i > 0)
    def _():
        block(i - 1, 2, 0)

    @pl.when(i == 0)
    def _():
        block(0, 1, KEY_BLOCK)

    for h in range(ATTN_HEADS):
        r0 = h * ATT_QB
        inv = 1.0 / jnp.sum(l_ref[r0:r0 + ATT_QB, :], axis=-1, keepdims=True)
        o = acc_ref[r0:r0 + ATT_QB, :] * inv
        y = _dot(o.astype(BF16), wuv_ref[h])
        out_ref[0, :, h * HEAD_DIM:(h + 1) * HEAD_DIM] = _rms(y, og_ref[h:h + 1, :]).astype(BF16)


def _attn_call(q_abs, ckv_t, ckv, bias, tb, w_uv, o_g):
    bn, _, s, _ = q_abs.shape
    nkb = s // KEY_BLOCK
    rows = ATTN_HEADS * ATT_QB
    return pl.pallas_call(
        _attn_kernel,
        grid=(bn, s // ATT_QB),
        in_specs=[pl.BlockSpec((1, ATTN_HEADS, ATT_QB, KV_LORA), lambda b, i: (b, 0, i, 0)),
                  pl.BlockSpec((1, nkb, KV_LORA, KEY_BLOCK), lambda b, i: (b, 0, 0, 0)),
                  pl.BlockSpec((1, s, KV_LORA), lambda b, i: (b, 0, 0)),
                  pl.BlockSpec((1, nkb, ATT_QB, KEY_BLOCK), lambda b, i: (b, 0, i, 0)),
                  _const_spec((ATTN_HEADS, ATT_QB, 2 * KEY_BLOCK)),
                  _const_spec((ATTN_HEADS, KV_LORA, HEAD_DIM)),
                  _const_spec((ATTN_HEADS, HEAD_DIM))],
        out_specs=pl.BlockSpec((1, ATT_QB, ATTN_HEADS * HEAD_DIM), lambda b, i: (b, i, 0)),
        out_shape=jax.ShapeDtypeStruct((bn, s, ATTN_HEADS * HEAD_DIM), BF16),
        scratch_shapes=[pltpu.VMEM((rows, LANES), F32),
                        pltpu.VMEM((rows, LANES), F32),
                        pltpu.VMEM((rows, KV_LORA), F32)],
        compiler_params=_params(("parallel", "arbitrary")),
        name="attn",
    )(q_abs, ckv_t, ckv, bias, tb, w_uv, o_g)


def _out_kernel(x_ref, ya_ref, yc_ref, w_ref, mod_ref, g2_ref, x1_ref, h2_ref):
    half = w_ref.shape[0] // 2
    y = _dot(ya_ref[0], w_ref[0:half, :]) + _dot(yc_ref[0], w_ref[half:, :])
    x1 = x_ref[0] + mod_ref[0, 2:3, :] * y
    x1_ref[0] = x1
    h2_ref[0] = (_rms(x1, g2_ref[...]) * (1.0 + mod_ref[0, 4:5, :]) + mod_ref[0, 3:4, :]).astype(BF16)


def _out_call(x, y_attn, y_conv, w_out, mod3, norm2_g, tm):
    bn, s, d = x.shape
    half = y_attn.shape[-1]
    tok = lambda w: pl.BlockSpec((1, tm, w), lambda b, i: (b, i, 0))
    return pl.pallas_call(
        _out_kernel,
        grid=(bn, s // tm),
        in_specs=[tok(d), tok(half), tok(half),
                  _const_spec((2 * half, d)),
                  pl.BlockSpec((1, 6, d), lambda b, i: (b, 0, 0)),
                  _const_spec((1, d))],
        out_specs=[tok(d), tok(d)],
        out_shape=[jax.ShapeDtypeStruct((bn, s, d), F32),
                   jax.ShapeDtypeStruct((bn, s, d), BF16)],
        compiler_params=_params(("parallel", "parallel")),
        name="out_proj",
    )(x, y_attn, y_conv, w_out, mod3, norm2_g)


MLP_SUBCHUNKS = 4
MLP_VMEM_LIMIT = 62 * 1024 * 1024


def _mlp_kernel(h2_ref, x1_ref, w1_ref, b1_ref, w2_ref, b2_ref, mod_ref, o_ref):
    f = pl.program_id(2)

    @pl.when(f == 0)
    def _():
        o_ref[0] = jnp.zeros(o_ref.shape[1:], F32)

    sub = w1_ref.shape[1] // MLP_SUBCHUNKS
    contrib = None
    for j in range(MLP_SUBCHUNKS):
        a = jnp.maximum(_dot(h2_ref[0], w1_ref[:, j * sub:(j + 1) * sub]) + b1_ref[:, j * sub:(j + 1) * sub], 0.0)
        part = _dot((a * a).astype(BF16), w2_ref[j * sub:(j + 1) * sub, :])
        contrib = part if contrib is None else contrib + part
    o_ref[0] += contrib

    @pl.when(f == pl.num_programs(2) - 1)
    def _():
        o_ref[0] = x1_ref[0] + mod_ref[0, 5:6, :] * (o_ref[0] + b2_ref[...])


def _mlp_call(h2, x1, w1, b1, w2, b2, mod3, tm, tf):
    bn, s, d = x1.shape
    dff = w1.shape[1]
    return pl.pallas_call(
        _mlp_kernel,
        grid=(bn, s // tm, dff // tf),
        in_specs=[pl.BlockSpec((1, tm, d), lambda b, i, f: (b, i, 0)),
                  pl.BlockSpec((1, tm, d), lambda b, i, f: (b, i, 0)),
                  pl.BlockSpec((d, tf), lambda b, i, f: (0, f)),
                  pl.BlockSpec((1, tf), lambda b, i, f: (0, f)),
                  pl.BlockSpec((tf, d), lambda b, i, f: (f, 0)),
                  pl.BlockSpec((1, d), lambda b, i, f: (0, 0)),
                  pl.BlockSpec((1, 6, d), lambda b, i, f: (b, 0, 0))],
        out_specs=pl.BlockSpec((1, tm, d), lambda b, i, f: (b, i, 0)),
        out_shape=jax.ShapeDtypeStruct((bn, s, d), F32),
        compiler_params=_params(("parallel", "parallel", "arbitrary"), MLP_VMEM_LIMIT),
        name="mlp",
    )(h2, x1, w1, b1, w2, b2, mod3)


def _t5_bucket_table():
    n = np.arange(2 * KEY_BLOCK, dtype=np.int64)
    max_exact = N_BUCKETS // 2
    nf = np.maximum(n, 1).astype(np.float32)
    ratio = np.log(nf / np.float32(max_exact)) / np.float32(math.log(MAX_DISTANCE / max_exact))
    large = max_exact + (ratio * np.float32(N_BUCKETS - max_exact)).astype(np.int32)
    large = np.minimum(large, N_BUCKETS - 1)
    bucket = np.where(n < max_exact, n, large)
    assert (bucket[MAX_DISTANCE:] == N_BUCKETS - 1).all()
    return bucket


def _near_bias_table(rel_bias):
    r = np.arange(ATT_QB)[:, None]
    c = np.arange(2 * KEY_BLOCK)[None, :]
    dist = np.clip(KEY_BLOCK + r - c, 0, 2 * KEY_BLOCK - 1)
    idx = jnp.asarray(_t5_bucket_table()[dist])
    rel_t = rel_bias.astype(F32).T * LOG2E
    tb = jnp.zeros((rel_t.shape[0],) + idx.shape, F32)
    for b in range(N_BUCKETS - 1):
        tb = jnp.where(idx[None] == b, (rel_t[:, b] - rel_t[:, N_BUCKETS - 1])[:, None, None], tb)
    return tb


def _layer(x, mod, norm1_g, norm2_g, w_in, cq_g, w_uq, w_uk, kv_g, qa_g, w_uv, w_iq, ki_g, ki_b, tb,
           conv_w, conv_b, ao_g, co_g, w_out, w_mlp1, b_mlp1, w_mlp2, b_mlp2):
    bn, s, d = x.shape
    topk = min(IDX_TOPK_MAX, s // 4)
    nkb = s // KEY_BLOCK
    mod3 = mod.reshape(bn, 6, d)

    o1 = Q_LORA + KV_LORA
    o2 = o1 + IDX_DIM
    o3 = o2 + IDX_HEADS
    w_in_r = jnp.concatenate(
        [w_in[:, :o3], jnp.zeros((d, LANES - IDX_DIM - IDX_HEADS), F32), w_in[:, o3:]], axis=1).astype(BF16)
    pad_lanes = lambda v: jnp.pad(v, (0, LANES - v.shape[0])).reshape(1, LANES)
    w_iq_t = jnp.pad(w_iq.reshape(Q_LORA, IDX_HEADS, IDX_DIM).transpose(1, 2, 0),
                     ((0, 0), (0, LANES - IDX_DIM), (0, 0))).astype(BF16).reshape(
                         IDX_HEADS * LANES, Q_LORA)

    tm = min(512, s)
    cq, ckv, kidx, widx, y_conv = _in_call(
        x, mod3, norm1_g.reshape(1, d), w_in_r, cq_g.reshape(1, -1), kv_g.reshape(1, -1),
        pad_lanes(ki_g), pad_lanes(ki_b), conv_w, conv_b.reshape(1, -1), co_g.reshape(1, -1), tm)

    q_abs, iq_t = _q_call(cq, w_uq.astype(BF16), w_uk.astype(BF16), qa_g.reshape(1, -1), w_iq_t, tm)

    widx_t = widx[:, :, IDX_DIM:IDX_DIM + IDX_HEADS].swapaxes(1, 2)
    bias = _idx_call(iq_t, kidx, widx_t, topk)

    ckv_t = ckv.reshape(bn, nkb, KEY_BLOCK, KV_LORA).swapaxes(2, 3)
    y_attn = _attn_call(q_abs, ckv_t, ckv, bias, tb, w_uv.astype(BF16), ao_g)

    x1, h2 = _out_call(x, y_attn, y_conv, w_out.astype(BF16), mod3, norm2_g.reshape(1, d), tm)
    return _mlp_call(h2, x1, w_mlp1.astype(BF16), b_mlp1.reshape(1, -1), w_mlp2.astype(BF16),
                     b_mlp2.reshape(1, -1), mod3, tm, 2048)


def kernel(x, c, w_ada, b_ada, norm1_g, norm2_g, w_in, cq_norm_g, w_uq, w_uk, kv_norm_g, q_abs_norm_g,
           w_uv, w_iq, idx_k_norm_g, idx_k_norm_b, rel_bias, conv_w, conv_b, attn_out_norm_g,
           conv_out_norm_g, w_out, w_mlp1, b_mlp1, w_mlp2, b_mlp2):
    bn = x.shape[0]
    depth = w_ada.shape[0]
    assert x.shape[1] % (2 * KEY_BLOCK) == 0 and bn <= 8
    tb = _near_bias_table(rel_bias)
    c8 = jnp.pad(c, ((0, 8 - bn), (0, 0)))
    for l in range(depth):
        mod = _mod_call(c8, w_ada[l], b_ada[l].reshape(1, -1))[:bn]
        x = _layer(x, mod, norm1_g[l], norm2_g[l], w_in[l], cq_norm_g[l], w_uq[l], w_uk[l], kv_norm_g[l],
                   q_abs_norm_g[l], w_uv[l], w_iq[l], idx_k_norm_g[l], idx_k_norm_b[l], tb, conv_w[l],
                   conv_b[l], attn_out_norm_g[l], conv_out_norm_g[l], w_out[l], w_mlp1[l], b_mlp1[l],
                   w_mlp2[l], b_mlp2[l])
    return x
```
